```python
import jax, jax.numpy as jnp
from jax import lax
import numpy as np

D_MODEL = 1024
BATCH = 16
SEQ = 4096
DEPTH = 4
DEC_BATCH = 16
DEC_SEQ = 64
PAST_LEN = 1024

CHUNK = 64
N_HEADS = 8
QK_NOPE_DIM = 128
QK_ROPE_DIM = 64
V_HEAD_DIM = 128
Q_LORA_RANK = 384
KV_LORA_RANK = 256
ROPE_BASE = 10000.0
D_CONV = D_MODEL
CONV_WIDTH = 3
D_FF = 2816
Q_BLOCK = 128
NORM_EPS = 1e-5
ATTN_SCALE = (QK_NOPE_DIM + QK_ROPE_DIM) ** -0.5
DEEPNORM_ALPHA = (2 * DEPTH) ** 0.25
DEEPNORM_BETA = (8 * DEPTH) ** -0.25
IN_COLS = Q_LORA_RANK + KV_LORA_RANK + QK_ROPE_DIM + 3 * D_CONV + 2 * D_MODEL

kernel_name = "hybrid_shortconv_mla_streaming_step"


def layer_norm(x, g, b):
    xf = x.astype(jnp.float32)
    mu = jnp.mean(xf, axis=-1, keepdims=True)
    var = jnp.mean(jnp.square(xf - mu), axis=-1, keepdims=True)
    return ((xf - mu) * lax.rsqrt(var + NORM_EPS) * g.astype(jnp.float32) + b.astype(jnp.float32)).astype(x.dtype)


def rms_norm(x, g):
    xf = x.astype(jnp.float32)
    ms = jnp.mean(jnp.square(xf), axis=-1, keepdims=True)
    return (xf * lax.rsqrt(ms + NORM_EPS) * g.astype(jnp.float32)).astype(x.dtype)


def swiglu_ffn(x, w_gate_up, w_down):
    gate, up = jnp.split(x @ w_gate_up, 2, axis=-1)
    return (jax.nn.silu(gate) * up) @ w_down


def apply_rope(x, pos):
    half = QK_ROPE_DIM // 2
    inv = ROPE_BASE ** (-jnp.arange(half, dtype=jnp.float32) / half)
    ang = pos.astype(jnp.float32)[:, None] * inv[None, :]
    shape = (1, ang.shape[0]) + (1,) * (x.ndim - 3) + (half,)
    cos = jnp.cos(ang).reshape(shape)
    sin = jnp.sin(ang).reshape(shape)
    xf = x.astype(jnp.float32)
    x1, x2 = xf[..., :half], xf[..., half:]
    return jnp.concatenate([x1 * cos - x2 * sin, x1 * sin + x2 * cos], axis=-1).astype(x.dtype)


def mla_attend(q_nope, q_rope, q_pos, k_nope, k_rope, v, k_pos):
    s = (jnp.einsum("bqhn,bthn->bhqt", q_nope, k_nope).astype(jnp.float32)
         + jnp.einsum("bqhr,btr->bhqt", q_rope, k_rope).astype(jnp.float32)) * ATTN_SCALE
    mask = (k_pos[None, :] // CHUNK) <= (q_pos[:, None] // CHUNK)
    s = jnp.where(mask[None, None], s, -1e30)
    p = jax.nn.softmax(s, axis=-1).astype(v.dtype)
    return jnp.einsum("bhqt,bthv->bqhv", p, v)


def mixer(x, conv_buf, past_lat, past_kr, w_in, b_gate, q_norm_gain, kv_norm_gain,
          w_uq, w_ukv, w_mla_out, conv_w, w_conv_out, w_mix_out):
    bsz, t_new, _ = x.shape
    pos0 = past_lat.shape[1]
    split_at = np.cumsum([Q_LORA_RANK, KV_LORA_RANK, QK_ROPE_DIM, D_CONV, D_CONV, D_CONV, D_MODEL]).tolist()
    q_lat, kv_lat, k_r, conv_b, conv_c, conv_u, g_conv, g_mla = jnp.split(x @ w_in, split_at, axis=-1)

    u = conv_c * conv_u
    u_pad = jnp.concatenate([conv_buf.astype(u.dtype), u], axis=1)
    conv_out = conv_w[CONV_WIDTH - 1] * u_pad[:, CONV_WIDTH - 1:]
    for k in range(CONV_WIDTH - 1):
        conv_out = conv_out + conv_w[k] * u_pad[:, k:k + t_new]
    y_conv = (conv_b * conv_out) @ w_conv_out
    new_conv_buf = u_pad[:, -(CONV_WIDTH - 1):]

    q_pos = pos0 + jnp.arange(t_new)
    c_kv = rms_norm(kv_lat, kv_norm_gain)
    k_rope_new = apply_rope(k_r, q_pos)
    lat_all = jnp.concatenate([past_lat.astype(c_kv.dtype), c_kv], axis=1)
    kr_all = jnp.concatenate([past_kr.astype(k_rope_new.dtype), k_rope_new], axis=1)
    t_k = lat_all.shape[1]
    k_pos = jnp.arange(t_k)
    kv = (lat_all @ w_ukv).reshape(bsz, t_k, N_HEADS, QK_NOPE_DIM + V_HEAD_DIM)
    k_nope, v = kv[..., :QK_NOPE_DIM], kv[..., QK_NOPE_DIM:]
    q = (rms_norm(q_lat, q_norm_gain) @ w_uq).reshape(bsz, t_new, N_HEADS, QK_NOPE_DIM + QK_ROPE_DIM)
    q_nope = q[..., :QK_NOPE_DIM]
    q_rope = apply_rope(q[..., QK_NOPE_DIM:], q_pos)
    if t_new > Q_BLOCK and t_new % Q_BLOCK == 0:
        n_blk = t_new // Q_BLOCK
        qn_b = q_nope.reshape(bsz, n_blk, Q_BLOCK, N_HEADS, QK_NOPE_DIM).transpose(1, 0, 2, 3, 4)
        qr_b = q_rope.reshape(bsz, n_blk, Q_BLOCK, N_HEADS, QK_ROPE_DIM).transpose(1, 0, 2, 3, 4)
        qp_b = q_pos.reshape(n_blk, Q_BLOCK)
        o = lax.map(lambda a: mla_attend(a[0], a[1], a[2], k_nope, kr_all, v, k_pos), (qn_b, qr_b, qp_b))
        o = o.transpose(1, 0, 2, 3, 4).reshape(bsz, t_new, N_HEADS, V_HEAD_DIM)
    else:
        o = mla_attend(q_nope, q_rope, q_pos, k_nope, kr_all, v, k_pos)
    y_mla = o.reshape(bsz, t_new, N_HEADS * V_HEAD_DIM) @ w_mla_out

    merged = jax.nn.sigmoid(g_conv + b_gate[0]) * y_conv + jax.nn.sigmoid(g_mla + b_gate[1]) * y_mla
    return merged @ w_mix_out, c_kv, k_rope_new, new_conv_buf


def run_trunk(x, cache_lat, cache_kr, conv_state, ffn1_w_gate_up, ffn1_w_down, ffn2_w_gate_up, ffn2_w_down,
              ln_gain, ln_bias, w_in, b_gate, q_norm_gain, kv_norm_gain, w_uq, w_ukv, w_mla_out,
              conv_w, w_conv_out, w_mix_out):
    bsz = x.shape[0]
    new_lat, new_kr, new_conv = [], [], []
    for l in range(DEPTH):
        if cache_lat is None:
            past_lat = jnp.zeros((bsz, 0, KV_LORA_RANK), x.dtype)
            past_kr = jnp.zeros((bsz, 0, QK_ROPE_DIM), x.dtype)
            buf = jnp.zeros((bsz, CONV_WIDTH - 1, D_CONV), x.dtype)
        else:
            past_lat, past_kr, buf = cache_lat[l], cache_kr[l], conv_state[l]
        x = layer_norm(DEEPNORM_ALPHA * x + 0.5 * swiglu_ffn(x, ffn1_w_gate_up[l], ffn1_w_down[l]),
                       ln_gain[l, 0], ln_bias[l, 0])
        m, c_kv, kr, cb = mixer(x, buf, past_lat, past_kr, w_in[l], b_gate[l], q_norm_gain[l], kv_norm_gain[l],
                                w_uq[l], w_ukv[l], w_mla_out[l], conv_w[l], w_conv_out[l], w_mix_out[l])
        x = layer_norm(DEEPNORM_ALPHA * x + m, ln_gain[l, 1], ln_bias[l, 1])
        x = layer_norm(DEEPNORM_ALPHA * x + 0.5 * swiglu_ffn(x, ffn2_w_gate_up[l], ffn2_w_down[l]),
                       ln_gain[l, 2], ln_bias[l, 2])
        new_lat.append(c_kv)
        new_kr.append(kr)
        new_conv.append(cb)
    return x, jnp.stack(new_lat), jnp.stack(new_kr), jnp.stack(new_conv)


def setup_inputs(seed: int = 0) -> dict:
    key = jax.random.key(seed)
    ks = jax.random.split(key, 24)

    def nrm(k, shape, scale):
        return jax.random.normal(k, shape, jnp.float32) * scale

    hq = N_HEADS * (QK_NOPE_DIM + QK_ROPE_DIM)
    hkv = N_HEADS * (QK_NOPE_DIM + V_HEAD_DIM)
    hv = N_HEADS * V_HEAD_DIM
    return {
        "x_prompt": nrm(ks[0], (BATCH, SEQ, D_MODEL), 1.0),
        "x_sample": nrm(ks[1], (DEC_BATCH, DEC_SEQ, D_MODEL), 1.0),
        "cache_kv_latent": nrm(ks[2], (DEPTH, DEC_BATCH, PAST_LEN, KV_LORA_RANK), 1.0),
        "cache_k_rope": nrm(ks[3], (DEPTH, DEC_BATCH, PAST_LEN, QK_ROPE_DIM), 1.0),
        "state_conv": nrm(ks[4], (DEPTH, DEC_BATCH, CONV_WIDTH - 1, D_CONV), 1.0),
        "ffn1_w_gate_up": nrm(ks[5], (DEPTH, D_MODEL, 2 * D_FF), D_MODEL ** -0.5),
        "ffn1_w_down": nrm(ks[6], (DEPTH, D_FF, D_MODEL), DEEPNORM_BETA * D_FF ** -0.5),
        "ffn2_w_gate_up": nrm(ks[7], (DEPTH, D_MODEL, 2 * D_FF), D_MODEL ** -0.5),
        "ffn2_w_down": nrm(ks[8], (DEPTH, D_FF, D_MODEL), DEEPNORM_BETA * D_FF ** -0.5),
        "ln_gain": 1.0 + nrm(ks[9], (DEPTH, 3, D_MODEL), 0.01),
        "ln_bias": nrm(ks[10], (DEPTH, 3, D_MODEL), 0.01),
        "w_in": nrm(ks[11], (DEPTH, D_MODEL, IN_COLS), D_MODEL ** -0.5),
        "b_gate": nrm(ks[12], (DEPTH, 2, D_MODEL), 0.01),
        "q_norm_gain": 1.0 + nrm(ks[13], (DEPTH, Q_LORA_RANK), 0.01),
        "kv_norm_gain": 1.0 + nrm(ks[14], (DEPTH, KV_LORA_RANK), 0.01),
        "w_uq": nrm(ks[15], (DEPTH, Q_LORA_RANK, hq), Q_LORA_RANK ** -0.5),
        "w_ukv": nrm(ks[16], (DEPTH, KV_LORA_RANK, hkv), KV_LORA_RANK ** -0.5),
        "w_mla_out": nrm(ks[17], (DEPTH, hv, D_MODEL), hv ** -0.5),
        "conv_w": nrm(ks[18], (DEPTH, CONV_WIDTH, D_CONV), CONV_WIDTH ** -0.5),
        "w_conv_out": nrm(ks[19], (DEPTH, D_CONV, D_MODEL), D_CONV ** -0.5),
        "w_mix_out": nrm(ks[20], (DEPTH, D_MODEL, D_MODEL), DEEPNORM_BETA * D_MODEL ** -0.5),
    }


def reference(x_prompt, x_sample, cache_kv_latent, cache_k_rope, state_conv,
              ffn1_w_gate_up, ffn1_w_down, ffn2_w_gate_up, ffn2_w_down, ln_gain, ln_bias,
              w_in, b_gate, q_norm_gain, kv_norm_gain, w_uq, w_ukv, w_mla_out,
              conv_w, w_conv_out, w_mix_out):
    y_prompt, p_lat, p_kr, p_conv = run_trunk(
        x_prompt, None, None, None, ffn1_w_gate_up, ffn1_w_down, ffn2_w_gate_up, ffn2_w_down,
        ln_gain, ln_bias, w_in, b_gate, q_norm_gain, kv_norm_gain, w_uq, w_ukv, w_mla_out,
        conv_w, w_conv_out, w_mix_out)
    y_sample, s_lat, s_kr, s_conv = run_trunk(
        x_sample, cache_kv_latent, cache_k_rope, state_conv, ffn1_w_gate_up, ffn1_w_down,
        ffn2_w_gate_up, ffn2_w_down, ln_gain, ln_bias, w_in, b_gate, q_norm_gain, kv_norm_gain,
        w_uq, w_ukv, w_mla_out, conv_w, w_conv_out, w_mix_out)
    return (y_prompt, y_sample, p_lat, p_kr, p_conv, s_lat, s_kr, s_conv)
```

```python
import functools

import jax
import jax.numpy as jnp
from jax import lax
from jax.experimental import pallas as pl
from jax.experimental.pallas import tpu as pltpu

D_MODEL = 1024
DEPTH = 4
CHUNK = 64
N_HEADS = 8
QK_NOPE_DIM = 128
QK_ROPE_DIM = 64
V_HEAD_DIM = 128
Q_LORA_RANK = 384
KV_LORA_RANK = 256
ROPE_BASE = 10000.0
D_CONV = D_MODEL
CONV_WIDTH = 3
D_FF = 2816
NORM_EPS = 1e-5
QK_DIM = QK_NOPE_DIM + QK_ROPE_DIM
ATTN_SCALE = QK_DIM ** -0.5
DEEPNORM_ALPHA = (2 * DEPTH) ** 0.25
MASK_VALUE = -1e30

Q_GROUP = QK_NOPE_DIM + 2 * QK_ROPE_DIM
KV_GROUP = QK_NOPE_DIM + V_HEAD_DIM
MLA_IN_COLS = Q_LORA_RANK + KV_LORA_RANK + 2 * QK_ROPE_DIM
CONV_HALO = 8

VMEM_LIMIT_BYTES = 56 * 1024 * 1024

F32 = jnp.float32
BF16 = jnp.bfloat16


def _dot(a, b):
    return jnp.dot(a, b, preferred_element_type=F32)


def _layer_norm(z, gain, bias):
    mu = jnp.mean(z, axis=-1, keepdims=True)
    zc = z - mu
    var = jnp.mean(zc * zc, axis=-1, keepdims=True)
    return zc * lax.rsqrt(var + NORM_EPS) * gain + bias


def _rms_norm(z, gain):
    ms = jnp.mean(z * z, axis=-1, keepdims=True)
    return z * lax.rsqrt(ms + NORM_EPS) * gain


def _resident(shape):
    zeros = (0,) * len(shape)
    return pl.BlockSpec(shape, lambda *_: zeros, pipeline_mode=pl.Buffered(1))


def _params(*semantics):
    return pltpu.CompilerParams(dimension_semantics=semantics, vmem_limit_bytes=VMEM_LIMIT_BYTES)


def _ffn_ln_kernel(x_ref, wg_ref, wu_ref, wd_ref, gain_ref, bias_ref, o_ref):
    x = x_ref[...]
    xb = x.astype(BF16)
    gate = _dot(xb, wg_ref[...])
    up = _dot(xb, wu_ref[...])
    hidden = (gate * jax.nn.sigmoid(gate) * up).astype(BF16)
    y = _dot(hidden, wd_ref[...])
    o_ref[...] = _layer_norm(DEEPNORM_ALPHA * x + 0.5 * y, gain_ref[...], bias_ref[...])


def _ffn_ln(x, wg, wu, wd, gain, bias, *, tm):
    rows = x.shape[0]
    row_spec = pl.BlockSpec((tm, D_MODEL), lambda i: (i, 0))
    return pl.pallas_call(
        _ffn_ln_kernel,
        grid=(rows // tm,),
        in_specs=[row_spec, _resident((D_MODEL, D_FF)), _resident((D_MODEL, D_FF)),
                  _resident((D_FF, D_MODEL)), _resident((1, D_MODEL)), _resident((1, D_MODEL))],
        out_specs=row_spec,
        out_shape=jax.ShapeDtypeStruct((rows, D_MODEL), F32),
        compiler_params=_params("parallel"),
        name="ffn_ln",
    )(x, wg, wu, wd, gain, bias)


def _rope_pair(z, table):
    w = z * table
    return w + pltpu.roll(w, QK_ROPE_DIM, axis=1)


def _store_kv_heads(kv, k_rope_bf16, kcat_ref, v_ref):
    for h in range(N_HEADS):
        base = h * KV_GROUP
        kcat_ref[h, :, 0:QK_NOPE_DIM] = kv[:, base:base + QK_NOPE_DIM].astype(BF16)
        kcat_ref[h, :, QK_NOPE_DIM:QK_DIM] = k_rope_bf16
        v_ref[h] = kv[:, base + QK_NOPE_DIM:base + KV_GROUP].astype(BF16)


def _mla_in_kernel(x_ref, rope_ref, wa_ref, qgain_ref, kvgain_ref, wuq_ref, wukv_ref,
                   q_ref, kcat_ref, v_ref, lat_ref, krope_ref):
    xb = x_ref[...].astype(BF16)
    table = rope_ref[...]
    h = _dot(xb, wa_ref[...])
    q_lat = h[:, 0:Q_LORA_RANK]
    kv_lat = h[:, Q_LORA_RANK:Q_LORA_RANK + KV_LORA_RANK]
    k_pair = h[:, Q_LORA_RANK + KV_LORA_RANK:MLA_IN_COLS]

    k_rope = _rope_pair(k_pair, table)[:, 0:QK_ROPE_DIM]
    krope_ref[...] = k_rope
    c_kv = _rms_norm(kv_lat, kvgain_ref[...])
    lat_ref[...] = c_kv
    kv = _dot(c_kv.astype(BF16), wukv_ref[...])
    _store_kv_heads(kv, k_rope.astype(BF16), kcat_ref, v_ref)

    qn = _rms_norm(q_lat, qgain_ref[...]).astype(BF16)
    q = _dot(qn, wuq_ref[...])
    for hd in range(N_HEADS):
        base = hd * Q_GROUP
        q_ref[hd, :, 0:QK_NOPE_DIM] = q[:, base:base + QK_NOPE_DIM].astype(BF16)
        roped = _rope_pair(q[:, base + QK_NOPE_DIM:base + Q_GROUP], table)
        q_ref[hd, :, QK_NOPE_DIM:QK_DIM] = roped[:, 0:QK_ROPE_DIM].astype(BF16)


def _mla_in(x, rope_table, wa, qgain, kvgain, wuq, wukv, *, batch, seq, tm):
    tiles = seq // tm
    rows = batch * seq

    def head_spec(width):
        return pl.BlockSpec((None, N_HEADS, tm, width), lambda i: (i // tiles, 0, i % tiles, 0))

    def head_shape(width):
        return jax.ShapeDtypeStruct((batch, N_HEADS, seq, width), BF16)

    return pl.pallas_call(
        _mla_in_kernel,
        grid=(rows // tm,),
        in_specs=[pl.BlockSpec((tm, D_MODEL), lambda i: (i, 0)),
                  pl.BlockSpec((tm, 2 * QK_ROPE_DIM), lambda i: (i % tiles, 0)),
                  _resident((D_MODEL, MLA_IN_COLS)), _resident((1, Q_LORA_RANK)), _resident((1, KV_LORA_RANK)),
                  _resident((Q_LORA_RANK, N_HEADS * Q_GROUP)), _resident((KV_LORA_RANK, N_HEADS * KV_GROUP))],
        out_specs=[head_spec(QK_DIM), head_spec(QK_DIM), head_spec(V_HEAD_DIM),
                   pl.BlockSpec((tm, KV_LORA_RANK), lambda i: (i, 0)),
                   pl.BlockSpec((tm, QK_ROPE_DIM), lambda i: (i, 0))],
        out_shape=[head_shape(QK_DIM), head_shape(QK_DIM), head_shape(V_HEAD_DIM),
                   jax.ShapeDtypeStruct((rows, KV_LORA_RANK), F32),
                   jax.ShapeDtypeStruct((rows, QK_ROPE_DIM), F32)],
        compiler_params=_params("parallel"),
        name="mla_in",
    )(x, rope_table, wa, qgain, kvgain, wuq, wukv)


def _kv_past_kernel(lat_ref, krope_ref, wukv_ref, kcat_ref, v_ref):
    kv = _dot(lat_ref[...].astype(BF16), wukv_ref[...])
    _store_kv_heads(kv, krope_ref[...].astype(BF16), kcat_ref, v_ref)


def _kv_past(lat, krope, wukv, *, batch, seq):
    def head_spec(width):
        return pl.BlockSpec((None, N_HEADS, seq, width), lambda b: (b, 0, 0, 0))

    return pl.pallas_call(
        _kv_past_kernel,
        grid=(batch,),
        in_specs=[pl.BlockSpec((seq, KV_LORA_RANK), lambda b: (b, 0)),
                  pl.BlockSpec((seq, QK_ROPE_DIM), lambda b: (b, 0)),
                  _resident((KV_LORA_RANK, N_HEADS * KV_GROUP))],
        out_specs=[head_spec(QK_DIM), head_spec(V_HEAD_DIM)],
        out_shape=[jax.ShapeDtypeStruct((batch, N_HEADS, seq, QK_DIM), BF16),
                   jax.ShapeDtypeStruct((batch, N_HEADS, seq, V_HEAD_DIM), BF16)],
        compiler_params=_params("parallel"),
        name="kv_past",
    )(lat, krope, wukv)


def _conv_in_kernel(x_ref, state_ref, wbcu_ref, wg_ref, convw_ref, bgate_ref, wco_ref,
                    mc_ref, gm_ref, newstate_ref, ubuf_ref, *, tm, tiles):
    xb = x_ref[...].astype(BF16)
    bcu = _dot(xb, wbcu_ref[...])
    conv_b = bcu[:, 0:D_CONV]
    u = bcu[:, D_CONV:2 * D_CONV] * bcu[:, 2 * D_CONV:3 * D_CONV]

    @pl.when(pl.program_id(0) % tiles == 0)
    def _():
        ubuf_ref[CONV_HALO - 2:CONV_HALO, :] = state_ref[...]

    ubuf_ref[CONV_HALO:CONV_HALO + tm, :] = u
    u_prev1 = ubuf_ref[CONV_HALO - 1:CONV_HALO - 1 + tm, :]
    u_prev2 = ubuf_ref[CONV_HALO - 2:CONV_HALO - 2 + tm, :]
    conv = convw_ref[2:3, :] * u + convw_ref[0:1, :] * u_prev2 + convw_ref[1:2, :] * u_prev1
    tail = u[tm - 2:tm, :]
    ubuf_ref[CONV_HALO - 2:CONV_HALO, :] = tail
    newstate_ref[...] = tail

    y_conv = _dot((conv_b * conv).astype(BF16), wco_ref[...])
    gates = _dot(xb, wg_ref[...])
    mc_ref[...] = jax.nn.sigmoid(gates[:, 0:D_MODEL] + bgate_ref[0:1, :]) * y_conv
    gm_ref[...] = jax.nn.sigmoid(gates[:, D_MODEL:2 * D_MODEL] + bgate_ref[1:2, :])


def _conv_in(x, state, wbcu, wg, convw, bgate, wco, *, batch, seq, tm):
    tiles = seq // tm
    rows = batch * seq
    row_spec = pl.BlockSpec((tm, D_MODEL), lambda i: (i, 0))
    state_spec = pl.BlockSpec((None, CONV_WIDTH - 1, D_CONV), lambda i: (i // tiles, 0, 0))
    return pl.pallas_call(
        functools.partial(_conv_in_kernel, tm=tm, tiles=tiles),
        grid=(rows // tm,),
        in_specs=[row_spec, state_spec,
                  _resident((D_MODEL, 3 * D_CONV)), _resident((D_MODEL, 2 * D_MODEL)),
                  _resident((CONV_WIDTH, D_CONV)), _resident((2, D_MODEL)), _resident((D_CONV, D_MODEL))],
        out_specs=[row_spec, row_spec, state_spec],
        out_shape=[jax.ShapeDtypeStruct((rows, D_MODEL), F32), jax.ShapeDtypeStruct((rows, D_MODEL), F32),
                   jax.ShapeDtypeStruct((batch, CONV_WIDTH - 1, D_CONV), F32)],
        scratch_shapes=[pltpu.VMEM((CONV_HALO + tm, D_CONV), F32)],
        compiler_params=_params("arbitrary"),
        name="conv_in",
    )(x, state, wbcu, wg, convw, bgate, wco)


def _scores(q, k):
    return lax.dot_general(q, k, (((1,), (1,)), ((), ())), preferred_element_type=F32) * ATTN_SCALE


def _chunk_mask(q_pos0, k_pos0, tq, tk):
    q_chunk = (q_pos0 + lax.broadcasted_iota(jnp.int32, (tq, tk), 0)) // CHUNK
    k_chunk = (k_pos0 + lax.broadcasted_iota(jnp.int32, (tq, tk), 1)) // CHUNK
    return k_chunk <= q_chunk


def _online_update(s, v, m_ref, l_ref, acc_ref):
    m_prev = m_ref[...]
    m_new = jnp.maximum(m_prev, jnp.max(s, axis=-1, keepdims=True))
    alpha = jnp.exp(m_prev - m_new)
    p = jnp.exp(s - m_new)
    l_ref[...] = alpha * l_ref[...] + jnp.sum(p, axis=-1, keepdims=True)
    acc_ref[...] = alpha * acc_ref[...] + _dot(p.astype(BF16), v)
    m_ref[...] = m_new


def _prompt_attn_kernel(q_ref, k_ref, v_ref, o_ref, m_ref, l_ref, acc_ref, *, tile):
    qi = pl.program_id(2)
    q = q_ref[...]
    m_ref[...] = jnp.full_like(m_ref, MASK_VALUE)
    l_ref[...] = jnp.zeros_like(l_ref)
    acc_ref[...] = jnp.zeros_like(acc_ref)

    def visible_tile(j, carry):
        start = pl.multiple_of(j * tile, tile)
        s = _scores(q, k_ref[pl.ds(start, tile), :])
        _online_update(s, v_ref[pl.ds(start, tile), :], m_ref, l_ref, acc_ref)
        return carry

    lax.fori_loop(0, qi, visible_tile, 0)

    start = pl.multiple_of(qi * tile, tile)
    s = _scores(q, k_ref[pl.ds(start, tile), :])
    s = jnp.where(_chunk_mask(0, 0, tile, tile), s, MASK_VALUE)
    _online_update(s, v_ref[pl.ds(start, tile), :], m_ref, l_ref, acc_ref)
    o_ref[...] = (acc_ref[...] * (1.0 / l_ref[...])).astype(o_ref.dtype)


def _prompt_attn(q, k, v, *, tile):
    batch, heads, seq, _ = q.shape
    tiles = seq // tile
    return pl.pallas_call(
        functools.partial(_prompt_attn_kernel, tile=tile),
        grid=(batch, heads, tiles),
        in_specs=[pl.BlockSpec((None, None, tile, QK_DIM), lambda b, h, i: (b, h, i, 0)),
                  pl.BlockSpec((None, None, seq, QK_DIM), lambda b, h, i: (b, h, 0, 0)),
                  pl.BlockSpec((None, None, seq, V_HEAD_DIM), lambda b, h, i: (b, h, 0, 0))],
        out_specs=pl.BlockSpec((tile, V_HEAD_DIM), lambda b, h, i: (b * tiles + i, h)),
        out_shape=jax.ShapeDtypeStruct((batch * seq, heads * V_HEAD_DIM), BF16),
        scratch_shapes=[pltpu.VMEM((tile, 1), F32), pltpu.VMEM((tile, 1), F32),
                        pltpu.VMEM((tile, V_HEAD_DIM), F32)],
        compiler_params=_params("parallel", "parallel", "arbitrary"),
        name="prompt_attn",
    )(q, k, v)


def _sample_attn_kernel(q_ref, kp_ref, vp_ref, kn_ref, vn_ref, o_ref, *, past, new):
    mask_past = _chunk_mask(past, 0, new, past)
    mask_new = _chunk_mask(past, past, new, new)
    for h in range(N_HEADS):
        q = q_ref[h]
        s_past = jnp.where(mask_past, _scores(q, kp_ref[h]), MASK_VALUE)
        s_new = jnp.where(mask_new, _scores(q, kn_ref[h]), MASK_VALUE)
        m = jnp.maximum(jnp.max(s_past, axis=-1, keepdims=True), jnp.max(s_new, axis=-1, keepdims=True))
        p_past = jnp.exp(s_past - m)
        p_new = jnp.exp(s_new - m)
        denom = jnp.sum(p_past, axis=-1, keepdims=True) + jnp.sum(p_new, axis=-1, keepdims=True)
        o = _dot(p_past.astype(BF16), vp_ref[h]) + _dot(p_new.astype(BF16), vn_ref[h])
        o_ref[:, h * V_HEAD_DIM:(h + 1) * V_HEAD_DIM] = (o * (1.0 / denom)).astype(o_ref.dtype)


def _sample_attn(q, k_past, v_past, k_new, v_new):
    batch, heads, new, _ = q.shape
    past = k_past.shape[2]

    def spec(rows, width):
        return pl.BlockSpec((None, heads, rows, width), lambda b: (b, 0, 0, 0))

    return pl.pallas_call(
        functools.partial(_sample_attn_kernel, past=past, new=new),
        grid=(batch,),
        in_specs=[spec(new, QK_DIM), spec(past, QK_DIM), spec(past, V_HEAD_DIM),
                  spec(new, QK_DIM), spec(new, V_HEAD_DIM)],
        out_specs=pl.BlockSpec((new, heads * V_HEAD_DIM), lambda b: (b, 0)),
        out_shape=jax.ShapeDtypeStruct((batch * new, heads * V_HEAD_DIM), BF16),
        compiler_params=_params("parallel"),
        name="sample_attn",
    )(q, k_past, v_past, k_new, v_new)


def _mix_ln_kernel(o_ref, mc_ref, gm_ref, x_ref, wmo_ref, wmix_ref, gain_ref, bias_ref, out_ref):
    y_mla = _dot(o_ref[...], wmo_ref[...])
    merged = (mc_ref[...] + gm_ref[...] * y_mla).astype(BF16)
    mixed = _dot(merged, wmix_ref[...])
    out_ref[...] = _layer_norm(DEEPNORM_ALPHA * x_ref[...] + mixed, gain_ref[...], bias_ref[...])


def _mix_ln(o, mc, gm, x, wmo, wmix, gain, bias, *, tm):
    rows = x.shape[0]
    row_spec = pl.BlockSpec((tm, D_MODEL), lambda i: (i, 0))
    return pl.pallas_call(
        _mix_ln_kernel,
        grid=(rows // tm,),
        in_specs=[row_spec, row_spec, row_spec, row_spec,
                  _resident((N_HEADS * V_HEAD_DIM, D_MODEL)), _resident((D_MODEL, D_MODEL)),
                  _resident((1, D_MODEL)), _resident((1, D_MODEL))],
        out_specs=row_spec,
        out_shape=jax.ShapeDtypeStruct((rows, D_MODEL), F32),
        compiler_params=_params("parallel"),
        name="mix_ln",
    )(o, mc, gm, x, wmo, wmix, gain, bias)


def _rope_table(pos0, length):
    half = QK_ROPE_DIM // 2
    inv = ROPE_BASE ** (-jnp.arange(half, dtype=F32) / half)
    ang = (pos0 + jnp.arange(length)).astype(F32)[:, None] * inv[None, :]
    cos, sin = jnp.cos(ang), jnp.sin(ang)
    return jnp.concatenate([cos, cos, sin, sin], axis=-1)


def _rotate_half_cols(w):
    half = QK_ROPE_DIM // 2
    return jnp.concatenate([-w[..., half:], w[..., :half]], axis=-1)


def _prepare_weights(ffn1_w_gate_up, ffn1_w_down, ffn2_w_gate_up, ffn2_w_down, w_in, w_uq, w_ukv,
                     w_mla_out, w_conv_out, w_mix_out):
    mla_cols = Q_LORA_RANK + KV_LORA_RANK + QK_ROPE_DIM
    k_rope_cols = w_in[:, :, Q_LORA_RANK + KV_LORA_RANK:mla_cols]
    wa = jnp.concatenate([w_in[:, :, :mla_cols], _rotate_half_cols(k_rope_cols)], axis=-1)
    uq = w_uq.reshape(DEPTH, Q_LORA_RANK, N_HEADS, QK_DIM)
    uq = jnp.concatenate([uq, _rotate_half_cols(uq[..., QK_NOPE_DIM:])], axis=-1)
    return dict(
        ffn1_g=ffn1_w_gate_up[:, :, :D_FF].astype(BF16), ffn1_u=ffn1_w_gate_up[:, :, D_FF:].astype(BF16),
        ffn1_d=ffn1_w_down.astype(BF16),
        ffn2_g=ffn2_w_gate_up[:, :, :D_FF].astype(BF16), ffn2_u=ffn2_w_gate_up[:, :, D_FF:].astype(BF16),
        ffn2_d=ffn2_w_down.astype(BF16),
        wa=wa.astype(BF16),
        wbcu=w_in[:, :, mla_cols:mla_cols + 3 * D_CONV].astype(BF16),
        wgates=w_in[:, :, mla_cols + 3 * D_CONV:].astype(BF16),
        wuq=uq.reshape(DEPTH, Q_LORA_RANK, N_HEADS * Q_GROUP).astype(BF16),
        wukv=w_ukv.astype(BF16), wmo=w_mla_out.astype(BF16), wco=w_conv_out.astype(BF16),
        wmix=w_mix_out.astype(BF16),
    )


def _run_trunk(x, cache_lat, cache_kr, conv_state, w, ln_gain, ln_bias, b_gate, q_norm_gain, kv_norm_gain,
               conv_w, *, tm, seq_tm, attn_tile):
    batch, seq, _ = x.shape
    past = 0 if cache_lat is None else cache_lat.shape[2]
    rope_table = _rope_table(past, seq)
    x = x.reshape(batch * seq, D_MODEL)
    new_lat, new_kr, new_conv = [], [], []
    for l in range(DEPTH):
        gain = lambda k: ln_gain[l, k][None, :]
        bias = lambda k: ln_bias[l, k][None, :]
        state = jnp.zeros((batch, CONV_WIDTH - 1, D_CONV), F32) if conv_state is None else conv_state[l]
        x = _ffn_ln(x, w["ffn1_g"][l], w["ffn1_u"][l], w["ffn1_d"][l], gain(0), bias(0), tm=tm)
        q, k_new, v_new, lat, kr = _mla_in(
            x, rope_table, w["wa"][l], q_norm_gain[l][None, :], kv_norm_gain[l][None, :],
            w["wuq"][l], w["wukv"][l], batch=batch, seq=seq, tm=seq_tm)
        mc, gm, conv_new = _conv_in(x, state, w["wbcu"][l], w["wgates"][l], conv_w[l], b_gate[l], w["wco"][l],
                                    batch=batch, seq=seq, tm=seq_tm)
        if cache_lat is None:
            o = _prompt_attn(q, k_new, v_new, tile=attn_tile)
        else:
            k_past, v_past = _kv_past(cache_lat[l].reshape(batch * past, KV_LORA_RANK),
                                      cache_kr[l].reshape(batch * past, QK_ROPE_DIM), w["wukv"][l],
                                      batch=batch, seq=past)
            o = _sample_attn(q, k_past, v_past, k_new, v_new)
        x = _mix_ln(o, mc, gm, x, w["wmo"][l], w["wmix"][l], gain(1), bias(1), tm=tm)
        x = _ffn_ln(x, w["ffn2_g"][l], w["ffn2_u"][l], w["ffn2_d"][l], gain(2), bias(2), tm=tm)
        new_lat.append(lat.reshape(batch, seq, KV_LORA_RANK))
        new_kr.append(kr.reshape(batch, seq, QK_ROPE_DIM))
        new_conv.append(conv_new)
    return x.reshape(batch, seq, D_MODEL), jnp.stack(new_lat), jnp.stack(new_kr), jnp.stack(new_conv)


def kernel(x_prompt, x_sample, cache_kv_latent, cache_k_rope, state_conv, ffn1_w_gate_up, ffn1_w_down,
           ffn2_w_gate_up, ffn2_w_down, ln_gain, ln_bias, w_in, b_gate, q_norm_gain, kv_norm_gain, w_uq, w_ukv,
           w_mla_out, conv_w, w_conv_out, w_mix_out):
    w = _prepare_weights(ffn1_w_gate_up, ffn1_w_down, ffn2_w_gate_up, ffn2_w_down, w_in, w_uq, w_ukv,
                         w_mla_out, w_conv_out, w_mix_out)
    shared = (w, ln_gain, ln_bias, b_gate, q_norm_gain, kv_norm_gain, conv_w)
    y_prompt, p_lat, p_kr, p_conv = _run_trunk(x_prompt, None, None, None, *shared,
                                               tm=512, seq_tm=512, attn_tile=512)
    y_sample, s_lat, s_kr, s_conv = _run_trunk(x_sample, cache_kv_latent, cache_k_rope, state_conv, *shared,
                                               tm=512, seq_tm=x_sample.shape[1], attn_tile=None)
    return (y_prompt, y_sample, p_lat, p_kr, p_conv, s_lat, s_kr, s_conv)
```

```python
import functools
import math

import jax
import jax.numpy as jnp
from jax import lax
from jax.experimental import pallas as pl
from jax.experimental.pallas import tpu as pltpu

D_MODEL = 1024
DEPTH = 4
CHUNK = 64
N_HEADS = 8
QK_NOPE_DIM = 128
QK_ROPE_DIM = 64
V_HEAD_DIM = 128
Q_LORA_RANK = 384
KV_LORA_RANK = 256
ROPE_BASE = 10000.0
D_CONV = D_MODEL
CONV_WIDTH = 3
D_FF = 2816
NORM_EPS = 1e-5
QK_DIM = QK_NOPE_DIM + QK_ROPE_DIM
ATTN_SCALE = QK_DIM ** -0.5
EXP2_SCALE = ATTN_SCALE * math.log2(math.e)
DEEPNORM_ALPHA = (2 * DEPTH) ** 0.25
MASK_VALUE = -1e30

Q_GROUP = QK_NOPE_DIM + 2 * QK_ROPE_DIM
MLA_IN_COLS = Q_LORA_RANK + KV_LORA_RANK + 2 * QK_ROPE_DIM
CONV_HALO = 8

VMEM_LIMIT_BYTES = 56 * 1024 * 1024

F32 = jnp.float32
BF16 = jnp.bfloat16


def _dot(a, b):
    return jnp.dot(a, b, preferred_element_type=F32)


def _dot_nt(a, b):
    return lax.dot_general(a, b, (((1,), (1,)), ((), ())), preferred_element_type=F32)


def _layer_norm(z, gain, bias):
    mu = jnp.mean(z, axis=-1, keepdims=True)
    zc = z - mu
    var = jnp.mean(zc * zc, axis=-1, keepdims=True)
    return zc * lax.rsqrt(var + NORM_EPS) * gain + bias


def _rms_norm(z, gain):
    ms = jnp.mean(z * z, axis=-1, keepdims=True)
    return z * lax.rsqrt(ms + NORM_EPS) * gain


def _resident(shape):
    zeros = (0,) * len(shape)
    return pl.BlockSpec(shape, lambda *_: zeros, pipeline_mode=pl.Buffered(1))


def _params(*semantics):
    return pltpu.CompilerParams(dimension_semantics=semantics, vmem_limit_bytes=VMEM_LIMIT_BYTES)


def _ffn_ln_kernel(x_ref, wg_ref, wu_ref, wd_ref, gain_ref, bias_ref, o_ref):
    x = x_ref[...]
    xb = x.astype(BF16)
    gate = _dot(xb, wg_ref[...])
    up = _dot(xb, wu_ref[...])
    hidden = (gate * jax.nn.sigmoid(gate) * up).astype(BF16)
    y = _dot(hidden, wd_ref[...])
    o_ref[...] = _layer_norm(DEEPNORM_ALPHA * x + 0.5 * y, gain_ref[...], bias_ref[...])


def _ffn_ln(x, wg, wu, wd, gain, bias, *, tm):
    rows = x.shape[0]
    row_spec = pl.BlockSpec((tm, D_MODEL), lambda i: (i, 0))
    return pl.pallas_call(
        _ffn_ln_kernel,
        grid=(rows // tm,),
        in_specs=[row_spec, _resident((D_MODEL, D_FF)), _resident((D_MODEL, D_FF)),
                  _resident((D_FF, D_MODEL)), _resident((1, D_MODEL)), _resident((1, D_MODEL))],
        out_specs=row_spec,
        out_shape=jax.ShapeDtypeStruct((rows, D_MODEL), F32),
        compiler_params=_params("parallel"),
        name="ffn_ln",
    )(x, wg, wu, wd, gain, bias)


def _store_kv_heads(lat_bf16, k_rope_bf16, wuk_ref, wuvt_ref, kcat_ref, vt_ref):
    k_nope = _dot(lat_bf16, wuk_ref[...])
    v_t = _dot_nt(wuvt_ref[...], lat_bf16)
    for h in range(N_HEADS):
        kcat_ref[h, :, 0:QK_NOPE_DIM] = k_nope[:, h * QK_NOPE_DIM:(h + 1) * QK_NOPE_DIM].astype(BF16)
        kcat_ref[h, :, QK_NOPE_DIM:QK_DIM] = k_rope_bf16
        vt_ref[h] = v_t[h * V_HEAD_DIM:(h + 1) * V_HEAD_DIM, :].astype(BF16)


def _mla_in_kernel(x_ref, rope_ref, ropet_ref, wa_ref, qgain_ref, kvgain_ref, wuqt_ref, wuk_ref, wuvt_ref,
                   qt_ref, kcat_ref, vt_ref, lat_ref, krope_ref):
    xb = x_ref[...].astype(BF16)
    h = _dot(xb, wa_ref[...])
    q_lat = h[:, 0:Q_LORA_RANK]
    kv_lat = h[:, Q_LORA_RANK:Q_LORA_RANK + KV_LORA_RANK]
    k_pair = h[:, Q_LORA_RANK + KV_LORA_RANK:MLA_IN_COLS]

    w = k_pair * rope_ref[...]
    k_rope = (w + pltpu.roll(w, QK_ROPE_DIM, axis=1))[:, 0:QK_ROPE_DIM]
    krope_ref[...] = k_rope
    c_kv = _rms_norm(kv_lat, kvgain_ref[...])
    lat_ref[...] = c_kv
    _store_kv_heads(c_kv.astype(BF16), k_rope.astype(BF16), wuk_ref, wuvt_ref, kcat_ref, vt_ref)

    qn = _rms_norm(q_lat, qgain_ref[...]).astype(BF16)
    q_t = _dot_nt(wuqt_ref[...], qn)
    cos_t = ropet_ref[0:QK_ROPE_DIM, :]
    sin_t = ropet_ref[QK_ROPE_DIM:2 * QK_ROPE_DIM, :]
    for hd in range(N_HEADS):
        base = hd * Q_GROUP
        rope0 = base + QK_NOPE_DIM
        qt_ref[hd, 0:QK_NOPE_DIM, :] = q_t[base:rope0, :].astype(BF16)
        roped = q_t[rope0:rope0 + QK_ROPE_DIM, :] * cos_t + q_t[rope0 + QK_ROPE_DIM:base + Q_GROUP, :] * sin_t
        qt_ref[hd, QK_NOPE_DIM:QK_DIM, :] = roped.astype(BF16)


def _mla_in(x, rope_table, rope_table_t, wa, qgain, kvgain, wuqt, wuk, wuvt, *, batch, seq, tm):
    tiles = seq // tm
    rows = batch * seq
    return pl.pallas_call(
        _mla_in_kernel,
        grid=(rows // tm,),
        in_specs=[pl.BlockSpec((tm, D_MODEL), lambda i: (i, 0)),
                  pl.BlockSpec((tm, 2 * QK_ROPE_DIM), lambda i: (i % tiles, 0)),
                  pl.BlockSpec((2 * QK_ROPE_DIM, tm), lambda i: (0, i % tiles)),
                  _resident((D_MODEL, MLA_IN_COLS)), _resident((1, Q_LORA_RANK)), _resident((1, KV_LORA_RANK)),
                  _resident((N_HEADS * Q_GROUP, Q_LORA_RANK)),
                  _resident((KV_LORA_RANK, N_HEADS * QK_NOPE_DIM)),
                  _resident((N_HEADS * V_HEAD_DIM, KV_LORA_RANK))],
        out_specs=[pl.BlockSpec((None, N_HEADS, QK_DIM, tm), lambda i: (i // tiles, 0, 0, i % tiles)),
                   pl.BlockSpec((None, N_HEADS, tm, QK_DIM), lambda i: (i // tiles, 0, i % tiles, 0)),
                   pl.BlockSpec((None, N_HEADS, None, V_HEAD_DIM, tm), lambda i: (i // tiles, 0, i % tiles, 0, 0)),
                   pl.BlockSpec((tm, KV_LORA_RANK), lambda i: (i, 0)),
                   pl.BlockSpec((tm, QK_ROPE_DIM), lambda i: (i, 0))],
        out_shape=[jax.ShapeDtypeStruct((batch, N_HEADS, QK_DIM, seq), BF16),
                   jax.ShapeDtypeStruct((batch, N_HEADS, seq, QK_DIM), BF16),
                   jax.ShapeDtypeStruct((batch, N_HEADS, tiles, V_HEAD_DIM, tm), BF16),
                   jax.ShapeDtypeStruct((rows, KV_LORA_RANK), F32),
                   jax.ShapeDtypeStruct((rows, QK_ROPE_DIM), F32)],
        compiler_params=_params("parallel"),
        name="mla_in",
    )(x, rope_table, rope_table_t, wa, qgain, kvgain, wuqt, wuk, wuvt)


def _kv_past_kernel(lat_ref, krope_ref, wuk_ref, wuvt_ref, kcat_ref, vt_ref):
    _store_kv_heads(lat_ref[...].astype(BF16), krope_ref[...].astype(BF16), wuk_ref, wuvt_ref, kcat_ref, vt_ref)


def _kv_past(lat, krope, wuk, wuvt, *, batch, seq):
    return pl.pallas_call(
        _kv_past_kernel,
        grid=(batch,),
        in_specs=[pl.BlockSpec((seq, KV_LORA_RANK), lambda b: (b, 0)),
                  pl.BlockSpec((seq, QK_ROPE_DIM), lambda b: (b, 0)),
                  _resident((KV_LORA_RANK, N_HEADS * QK_NOPE_DIM)),
                  _resident((N_HEADS * V_HEAD_DIM, KV_LORA_RANK))],
        out_specs=[pl.BlockSpec((None, N_HEADS, seq, QK_DIM), lambda b: (b, 0, 0, 0)),
                   pl.BlockSpec((None, N_HEADS, V_HEAD_DIM, seq), lambda b: (b, 0, 0, 0))],
        out_shape=[jax.ShapeDtypeStruct((batch, N_HEADS, seq, QK_DIM), BF16),
                   jax.ShapeDtypeStruct((batch, N_HEADS, V_HEAD_DIM, seq), BF16)],
        compiler_params=_params("parallel"),
        name="kv_past",
    )(lat, krope, wuk, wuvt)


def _conv_in_kernel(x_ref, state_ref, wbcu_ref, wg_ref, convw_ref, bgate_ref, wco_ref,
                    mc_ref, gm_ref, newstate_ref, ubuf_ref, *, tm, tiles):
    xb = x_ref[...].astype(BF16)
    bcu = _dot(xb, wbcu_ref[...])
    conv_b = bcu[:, 0:D_CONV]
    u = bcu[:, D_CONV:2 * D_CONV] * bcu[:, 2 * D_CONV:3 * D_CONV]

    @pl.when(pl.program_id(0) % tiles == 0)
    def _():
        ubuf_ref[CONV_HALO - 2:CONV_HALO, :] = state_ref[...]

    ubuf_ref[CONV_HALO:CONV_HALO + tm, :] = u
    u_prev1 = ubuf_ref[CONV_HALO - 1:CONV_HALO - 1 + tm, :]
    u_prev2 = ubuf_ref[CONV_HALO - 2:CONV_HALO - 2 + tm, :]
    conv = convw_ref[2:3, :] * u + convw_ref[0:1, :] * u_prev2 + convw_ref[1:2, :] * u_prev1
    tail = u[tm - 2:tm, :]
    ubuf_ref[CONV_HALO - 2:CONV_HALO, :] = tail
    newstate_ref[...] = tail

    y_conv = _dot((conv_b * conv).astype(BF16), wco_ref[...])
    gates = _dot(xb, wg_ref[...])
    mc_ref[...] = jax.nn.sigmoid(gates[:, 0:D_MODEL] + bgate_ref[0:1, :]) * y_conv
    gm_ref[...] = jax.nn.sigmoid(gates[:, D_MODEL:2 * D_MODEL] + bgate_ref[1:2, :])


def _conv_in(x, state, wbcu, wg, convw, bgate, wco, *, batch, seq, tm):
    tiles = seq // tm
    rows = batch * seq
    row_spec = pl.BlockSpec((tm, D_MODEL), lambda i: (i, 0))
    state_spec = pl.BlockSpec((None, CONV_WIDTH - 1, D_CONV), lambda i: (i // tiles, 0, 0))
    return pl.pallas_call(
        functools.partial(_conv_in_kernel, tm=tm, tiles=tiles),
        grid=(rows // tm,),
        in_specs=[row_spec, state_spec,
                  _resident((D_MODEL, 3 * D_CONV)), _resident((D_MODEL, 2 * D_MODEL)),
                  _resident((CONV_WIDTH, D_CONV)), _resident((2, D_MODEL)), _resident((D_CONV, D_MODEL))],
        out_specs=[row_spec, row_spec, state_spec],
        out_shape=[jax.ShapeDtypeStruct((rows, D_MODEL), F32), jax.ShapeDtypeStruct((rows, D_MODEL), F32),
                   jax.ShapeDtypeStruct((batch, CONV_WIDTH - 1, D_CONV), F32)],
        scratch_shapes=[pltpu.VMEM((CONV_HALO + tm, D_CONV), F32)],
        compiler_params=_params("arbitrary"),
        name="conv_in",
    )(x, state, wbcu, wg, convw, bgate, wco)


def _chunk_mask_t(k_pos0, q_pos0, tk, tq):
    k_chunk = (k_pos0 + lax.broadcasted_iota(jnp.int32, (tk, tq), 0)) // CHUNK
    q_chunk = (q_pos0 + lax.broadcasted_iota(jnp.int32, (tk, tq), 1)) // CHUNK
    return k_chunk <= q_chunk


def _prompt_attn_kernel(qt_ref, k_ref, vt_ref, o_ref, s_ref, acc_ref, *, tile, group):
    qi = pl.program_id(2)

    def scores(g, j):
        start = pl.multiple_of(j * tile, tile)
        return _dot(k_ref[g, pl.ds(start, tile), :], qt_ref[g]) * EXP2_SCALE

    def visible_tile(j, col_max):
        out = []
        for g in range(group):
            s_t = scores(g, j)
            s_ref[g, j] = s_t
            out.append(jnp.maximum(col_max[g], jnp.max(s_t, axis=0, keepdims=True)))
        return tuple(out)

    col_max = lax.fori_loop(0, qi, visible_tile,
                            tuple(jnp.full((1, tile), MASK_VALUE, F32) for _ in range(group)))
    mask = _chunk_mask_t(0, 0, tile, tile)
    final_max = []
    for g in range(group):
        s_t = jnp.where(mask, scores(g, qi), MASK_VALUE)
        s_ref[g, qi] = s_t
        final_max.append(jnp.maximum(col_max[g], jnp.max(s_t, axis=0, keepdims=True)))

    acc_ref[...] = jnp.zeros_like(acc_ref)

    def weighted_values(j, col_sum):
        out = []
        for g in range(group):
            p_t = jnp.exp2(s_ref[g, j] - final_max[g])
            out.append(col_sum[g] + jnp.sum(p_t, axis=0, keepdims=True))
            acc_ref[g] += _dot(vt_ref[g, j], p_t.astype(BF16))
        return tuple(out)

    col_sum = lax.fori_loop(0, qi + 1, weighted_values,
                            tuple(jnp.zeros((1, tile), F32) for _ in range(group)))
    for g in range(group):
        o_t = acc_ref[g] * (1.0 / col_sum[g])
        o_ref[:, g * V_HEAD_DIM:(g + 1) * V_HEAD_DIM] = o_t.T.astype(o_ref.dtype)


def _prompt_attn(q_t, k, v_t, *, tile, group):
    batch, heads, _, seq = q_t.shape
    tiles = seq // tile
    return pl.pallas_call(
        functools.partial(_prompt_attn_kernel, tile=tile, group=group),
        grid=(batch, heads // group, tiles),
        in_specs=[pl.BlockSpec((None, group, QK_DIM, tile), lambda b, h, i: (b, h, 0, i)),
                  pl.BlockSpec((None, group, seq, QK_DIM), lambda b, h, i: (b, h, 0, 0),
                               pipeline_mode=pl.Buffered(1)),
                  pl.BlockSpec((None, group, tiles, V_HEAD_DIM, tile), lambda b, h, i: (b, h, 0, 0, 0),
                               pipeline_mode=pl.Buffered(1))],
        out_specs=pl.BlockSpec((tile, group * V_HEAD_DIM), lambda b, h, i: (b * tiles + i, h)),
        out_shape=jax.ShapeDtypeStruct((batch * seq, heads * V_HEAD_DIM), BF16),
        scratch_shapes=[pltpu.VMEM((group, tiles, tile, tile), F32),
                        pltpu.VMEM((group, V_HEAD_DIM, tile), F32)],
        compiler_params=_params("parallel", "parallel", "arbitrary"),
        name="prompt_attn",
    )(q_t, k, v_t)


def _sample_attn_kernel(qt_ref, kp_ref, vpt_ref, kn_ref, vnt_ref, o_ref, *, past, new):
    mask_past = _chunk_mask_t(0, past, past, new)
    mask_new = _chunk_mask_t(past, past, new, new)
    for h in range(N_HEADS):
        q_t = qt_ref[h]
        s_past = jnp.where(mask_past, _dot(kp_ref[h], q_t), MASK_VALUE)
        s_new = jnp.where(mask_new, _dot(kn_ref[h], q_t), MASK_VALUE)
        m = jnp.maximum(jnp.max(s_past, axis=0, keepdims=True), jnp.max(s_new, axis=0, keepdims=True))
        p_past = jnp.exp2((s_past - m) * EXP2_SCALE)
        p_new = jnp.exp2((s_new - m) * EXP2_SCALE)
        denom = jnp.sum(p_past, axis=0, keepdims=True) + jnp.sum(p_new, axis=0, keepdims=True)
        o_t = _dot(vpt_ref[h], p_past.astype(BF16)) + _dot(vnt_ref[h, 0], p_new.astype(BF16))
        o_ref[h] = (o_t * (1.0 / denom)).astype(o_ref.dtype)


def _sample_attn(q_t, k_past, v_past_t, k_new, v_new_t):
    batch, heads, _, new = q_t.shape
    past = k_past.shape[2]

    def spec(*dims):
        return pl.BlockSpec((None,) + dims, lambda b: (b,) + (0,) * len(dims))

    return pl.pallas_call(
        functools.partial(_sample_attn_kernel, past=past, new=new),
        grid=(batch,),
        in_specs=[spec(heads, QK_DIM, new), spec(heads, past, QK_DIM), spec(heads, V_HEAD_DIM, past),
                  spec(heads, new, QK_DIM), spec(heads, 1, V_HEAD_DIM, new)],
        out_specs=spec(heads, V_HEAD_DIM, new),
        out_shape=jax.ShapeDtypeStruct((batch, heads, V_HEAD_DIM, new), BF16),
        compiler_params=_params("parallel"),
        name="sample_attn",
    )(q_t, k_past, v_past_t, k_new, v_new_t)


def _mix_ln_kernel(o_ref, mc_ref, gm_ref, x_ref, wmo_ref, wmix_ref, gain_ref, bias_ref, out_ref):
    y_mla = _dot(o_ref[...], wmo_ref[...])
    merged = (mc_ref[...] + gm_ref[...] * y_mla).astype(BF16)
    mixed = _dot(merged, wmix_ref[...])
    out_ref[...] = _layer_norm(DEEPNORM_ALPHA * x_ref[...] + mixed, gain_ref[...], bias_ref[...])


def _mix_ln(o, mc, gm, x, wmo, wmix, gain, bias, *, tm):
    rows = x.shape[0]
    row_spec = pl.BlockSpec((tm, D_MODEL), lambda i: (i, 0))
    return pl.pallas_call(
        _mix_ln_kernel,
        grid=(rows // tm,),
        in_specs=[row_spec, row_spec, row_spec, row_spec,
                  _resident((N_HEADS * V_HEAD_DIM, D_MODEL)), _resident((D_MODEL, D_MODEL)),
                  _resident((1, D_MODEL)), _resident((1, D_MODEL))],
        out_specs=row_spec,
        out_shape=jax.ShapeDtypeStruct((rows, D_MODEL), F32),
        compiler_params=_params("parallel"),
        name="mix_ln",
    )(o, mc, gm, x, wmo, wmix, gain, bias)


def _rope_tables(pos0, length):
    half = QK_ROPE_DIM // 2
    inv = ROPE_BASE ** (-jnp.arange(half, dtype=F32) / half)
    ang = (pos0 + jnp.arange(length)).astype(F32)[:, None] * inv[None, :]
    cos, sin = jnp.cos(ang), jnp.sin(ang)
    table = jnp.concatenate([cos, cos, sin, sin], axis=-1)
    return table, table.T


def _rotate_half_cols(w):
    half = QK_ROPE_DIM // 2
    return jnp.concatenate([-w[..., half:], w[..., :half]], axis=-1)


def _prepare_weights(ffn1_w_gate_up, ffn1_w_down, ffn2_w_gate_up, ffn2_w_down, w_in, w_uq, w_ukv,
                     w_mla_out, w_conv_out, w_mix_out):
    mla_cols = Q_LORA_RANK + KV_LORA_RANK + QK_ROPE_DIM
    k_rope_cols = w_in[:, :, Q_LORA_RANK + KV_LORA_RANK:mla_cols]
    wa = jnp.concatenate([w_in[:, :, :mla_cols], _rotate_half_cols(k_rope_cols)], axis=-1)
    uq = w_uq.reshape(DEPTH, Q_LORA_RANK, N_HEADS, QK_DIM)
    uq = jnp.concatenate([uq, _rotate_half_cols(uq[..., QK_NOPE_DIM:])], axis=-1)
    uq_t = uq.reshape(DEPTH, Q_LORA_RANK, N_HEADS * Q_GROUP).transpose(0, 2, 1)
    ukv = w_ukv.reshape(DEPTH, KV_LORA_RANK, N_HEADS, QK_NOPE_DIM + V_HEAD_DIM)
    uk = ukv[..., :QK_NOPE_DIM].reshape(DEPTH, KV_LORA_RANK, N_HEADS * QK_NOPE_DIM)
    uv_t = ukv[..., QK_NOPE_DIM:].reshape(DEPTH, KV_LORA_RANK, N_HEADS * V_HEAD_DIM).transpose(0, 2, 1)
    return dict(
        ffn1_g=ffn1_w_gate_up[:, :, :D_FF].astype(BF16), ffn1_u=ffn1_w_gate_up[:, :, D_FF:].astype(BF16),
        ffn1_d=ffn1_w_down.astype(BF16),
        ffn2_g=ffn2_w_gate_up[:, :, :D_FF].astype(BF16), ffn2_u=ffn2_w_gate_up[:, :, D_FF:].astype(BF16),
        ffn2_d=ffn2_w_down.astype(BF16),
        wa=wa.astype(BF16),
        wbcu=w_in[:, :, mla_cols:mla_cols + 3 * D_CONV].astype(BF16),
        wgates=w_in[:, :, mla_cols + 3 * D_CONV:].astype(BF16),
        wuqt=uq_t.astype(BF16), wuk=uk.astype(BF16), wuvt=uv_t.astype(BF16),
        wmo=w_mla_out.astype(BF16), wco=w_conv_out.astype(BF16), wmix=w_mix_out.astype(BF16),
    )


def _run_trunk(x, cache_lat, cache_kr, conv_state, w, ln_gain, ln_bias, b_gate, q_norm_gain, kv_norm_gain,
               conv_w, *, tm, seq_tm, attn_group):
    batch, seq, _ = x.shape
    past = 0 if cache_lat is None else cache_lat.shape[2]
    rope_table, rope_table_t = _rope_tables(past, seq)
    x = x.reshape(batch * seq, D_MODEL)
    new_lat, new_kr, new_conv = [], [], []
    for l in range(DEPTH):
        gain = lambda k: ln_gain[l, k][None, :]
        bias = lambda k: ln_bias[l, k][None, :]
        state = jnp.zeros((batch, CONV_WIDTH - 1, D_CONV), F32) if conv_state is None else conv_state[l]
        x = _ffn_ln(x, w["ffn1_g"][l], w["ffn1_u"][l], w["ffn1_d"][l], gain(0), bias(0), tm=tm)
        q_t, k_new, v_new_t, lat, kr = _mla_in(
            x, rope_table, rope_table_t, w["wa"][l], q_norm_gain[l][None, :], kv_norm_gain[l][None, :],
            w["wuqt"][l], w["wuk"][l], w["wuvt"][l], batch=batch, seq=seq, tm=seq_tm)
        mc, gm, conv_new = _conv_in(x, state, w["wbcu"][l], w["wgates"][l], conv_w[l], b_gate[l], w["wco"][l],
                                    batch=batch, seq=seq, tm=seq_tm)
        if cache_lat is None:
            o = _prompt_attn(q_t, k_new, v_new_t, tile=seq_tm, group=attn_group)
        else:
            k_past, v_past_t = _kv_past(cache_lat[l].reshape(batch * past, KV_LORA_RANK),
                                        cache_kr[l].reshape(batch * past, QK_ROPE_DIM),
                                        w["wuk"][l], w["wuvt"][l], batch=batch, seq=past)
            o_t = _sample_attn(q_t, k_past, v_past_t, k_new, v_new_t)
            o = o_t.transpose(0, 3, 1, 2).reshape(batch * seq, N_HEADS * V_HEAD_DIM)
        x = _mix_ln(o, mc, gm, x, w["wmo"][l], w["wmix"][l], gain(1), bias(1), tm=tm)
        x = _ffn_ln(x, w["ffn2_g"][l], w["ffn2_u"][l], w["ffn2_d"][l], gain(2), bias(2), tm=tm)
        new_lat.append(lat.reshape(batch, seq, KV_LORA_RANK))
        new_kr.append(kr.reshape(batch, seq, QK_ROPE_DIM))
        new_conv.append(conv_new)
    return x.reshape(batch, seq, D_MODEL), jnp.stack(new_lat), jnp.stack(new_kr), jnp.stack(new_conv)


def kernel(x_prompt, x_sample, cache_kv_latent, cache_k_rope, state_conv, ffn1_w_gate_up, ffn1_w_down,
           ffn2_w_gate_up, ffn2_w_down, ln_gain, ln_bias, w_in, b_gate, q_norm_gain, kv_norm_gain, w_uq, w_ukv,
           w_mla_out, conv_w, w_conv_out, w_mix_out):
    w = _prepare_weights(ffn1_w_gate_up, ffn1_w_down, ffn2_w_gate_up, ffn2_w_down, w_in, w_uq, w_ukv,
                         w_mla_out, w_conv_out, w_mix_out)
    shared = (w, ln_gain, ln_bias, b_gate, q_norm_gain, kv_norm_gain, conv_w)
    y_prompt, p_lat, p_kr, p_conv = _run_trunk(x_prompt, None, None, None, *shared,
                                               tm=512, seq_tm=512, attn_group=4)
    y_sample, s_lat, s_kr, s_conv = _run_trunk(x_sample, cache_kv_latent, cache_k_rope, state_conv, *shared,
                                               tm=512, seq_tm=x_sample.shape[1], attn_group=None)
    return (y_prompt, y_sample, p_lat, p_kr, p_conv, s_lat, s_kr, s_conv)
```

```python
import functools
import math

import jax
import jax.numpy as jnp
from jax import lax
from jax.experimental import pallas as pl
from jax.experimental.pallas import tpu as pltpu

D_MODEL = 1024
DEPTH = 4
CHUNK = 64
N_HEADS = 8
QK_NOPE_DIM = 128
QK_ROPE_DIM = 64
V_HEAD_DIM = 128
Q_LORA_RANK = 384
KV_LORA_RANK = 256
ROPE_BASE = 10000.0
D_CONV = D_MODEL
CONV_WIDTH = 3
D_FF = 2816
NORM_EPS = 1e-5
QK_DIM = QK_NOPE_DIM + QK_ROPE_DIM
ATTN_SCALE = QK_DIM ** -0.5
EXP2_SCALE = ATTN_SCALE * math.log2(math.e)
DEEPNORM_ALPHA = (2 * DEPTH) ** 0.25
MASK_VALUE = -1e30

Q_GROUP = QK_NOPE_DIM + 2 * QK_ROPE_DIM
MLA_IN_COLS = Q_LORA_RANK + KV_LORA_RANK + 2 * QK_ROPE_DIM
CONV_HALO = 8
BF16_SUBLANES = 16
V_ROWS = V_HEAD_DIM + BF16_SUBLANES

VMEM_LIMIT_BYTES = 56 * 1024 * 1024

F32 = jnp.float32
BF16 = jnp.bfloat16


def _dot(a, b):
    return jnp.dot(a, b, preferred_element_type=F32)


def _dot_nt(a, b):
    return lax.dot_general(a, b, (((1,), (1,)), ((), ())), preferred_element_type=F32)


def _layer_norm(z, gain, bias):
    mu = jnp.mean(z, axis=-1, keepdims=True)
    zc = z - mu
    var = jnp.mean(zc * zc, axis=-1, keepdims=True)
    return zc * lax.rsqrt(var + NORM_EPS) * gain + bias


def _rms_norm(z, gain):
    ms = jnp.mean(z * z, axis=-1, keepdims=True)
    return z * lax.rsqrt(ms + NORM_EPS) * gain


def _resident(shape):
    zeros = (0,) * len(shape)
    return pl.BlockSpec(shape, lambda *_: zeros, pipeline_mode=pl.Buffered(1))


def _stacked_out(layer, stacked, in_count, out_index):
    if layer == 0:
        return [], [], {}
    return [stacked], [pl.BlockSpec(memory_space=pl.ANY)], {in_count: out_index}


def _params(*semantics):
    return pltpu.CompilerParams(dimension_semantics=semantics, vmem_limit_bytes=VMEM_LIMIT_BYTES)


def _ffn_ln_kernel(x_ref, wg_ref, wu_ref, wd_ref, gain_ref, bias_ref, o_ref):
    x = x_ref[...]
    xb = x.astype(BF16)
    gate = _dot(xb, wg_ref[...])
    up = _dot(xb, wu_ref[...])
    hidden = (gate * jax.nn.sigmoid(gate) * up).astype(BF16)
    y = _dot(hidden, wd_ref[...])
    o_ref[...] = _layer_norm(DEEPNORM_ALPHA * x + 0.5 * y, gain_ref[...], bias_ref[...])


def _ffn_ln(x, wg, wu, wd, gain, bias, *, tm):
    rows = x.shape[0]
    row_spec = pl.BlockSpec((tm, D_MODEL), lambda i: (i, 0))
    return pl.pallas_call(
        _ffn_ln_kernel,
        grid=(rows // tm,),
        in_specs=[row_spec, _resident((D_MODEL, D_FF)), _resident((D_MODEL, D_FF)),
                  _resident((D_FF, D_MODEL)), _resident((1, D_MODEL)), _resident((1, D_MODEL))],
        out_specs=row_spec,
        out_shape=jax.ShapeDtypeStruct((rows, D_MODEL), F32),
        compiler_params=_params("parallel"),
        name="ffn_ln",
    )(x, wg, wu, wd, gain, bias)


def _store_kv_heads(lat_bf16, k_rope_bf16, wuk_ref, wuvt_ref, kcat_ref, vt_ref):
    k_nope = _dot(lat_bf16, wuk_ref[...])
    v_t = _dot_nt(wuvt_ref[...], lat_bf16)
    ones = jnp.ones((V_ROWS - V_HEAD_DIM, lat_bf16.shape[0]), BF16)
    for h in range(N_HEADS):
        kcat_ref[h, :, 0:QK_NOPE_DIM] = k_nope[:, h * QK_NOPE_DIM:(h + 1) * QK_NOPE_DIM].astype(BF16)
        kcat_ref[h, :, QK_NOPE_DIM:QK_DIM] = k_rope_bf16
        vt_ref[h, 0:V_HEAD_DIM, :] = v_t[h * V_HEAD_DIM:(h + 1) * V_HEAD_DIM, :].astype(BF16)
        vt_ref[h, V_HEAD_DIM:V_ROWS, :] = ones


def _mla_in_kernel(x_ref, rope_ref, ropet_ref, wa_ref, qgain_ref, kvgain_ref, wuqt_ref, wuk_ref, wuvt_ref,
                   *rest):
    qt_ref, kcat_ref, vt_ref, lat_ref, krope_ref = rest[-5:]
    xb = x_ref[...].astype(BF16)
    h = _dot(xb, wa_ref[...])
    q_lat = h[:, 0:Q_LORA_RANK]
    kv_lat = h[:, Q_LORA_RANK:Q_LORA_RANK + KV_LORA_RANK]
    k_pair = h[:, Q_LORA_RANK + KV_LORA_RANK:MLA_IN_COLS]

    w = k_pair * rope_ref[...]
    k_rope = (w + pltpu.roll(w, QK_ROPE_DIM, axis=1))[:, 0:QK_ROPE_DIM]
    krope_ref[...] = k_rope
    c_kv = _rms_norm(kv_lat, kvgain_ref[...])
    lat_ref[...] = c_kv
    _store_kv_heads(c_kv.astype(BF16), k_rope.astype(BF16), wuk_ref, wuvt_ref, kcat_ref, vt_ref)

    qn = _rms_norm(q_lat, qgain_ref[...]).astype(BF16)
    q_t = _dot_nt(wuqt_ref[...], qn) * EXP2_SCALE
    cos_t = ropet_ref[0:QK_ROPE_DIM, :]
    sin_t = ropet_ref[QK_ROPE_DIM:2 * QK_ROPE_DIM, :]
    for hd in range(N_HEADS):
        base = hd * Q_GROUP
        rope0 = base + QK_NOPE_DIM
        qt_ref[hd, 0:QK_NOPE_DIM, :] = q_t[base:rope0, :].astype(BF16)
        roped = q_t[rope0:rope0 + QK_ROPE_DIM, :] * cos_t + q_t[rope0 + QK_ROPE_DIM:base + Q_GROUP, :] * sin_t
        qt_ref[hd, QK_NOPE_DIM:QK_DIM, :] = roped.astype(BF16)


def _mla_in(x, rope_table, rope_table_t, wa, qgain, kvgain, wuqt, wuk, wuvt, lat_stack, krope_stack, *,
            layer, batch, seq, tm):
    tiles = seq // tm
    rows = batch * seq
    lat_extra, lat_specs, lat_alias = _stacked_out(layer, lat_stack, 9, 3)
    kr_extra, kr_specs, kr_alias = _stacked_out(layer, krope_stack, 10, 4)
    return pl.pallas_call(
        _mla_in_kernel,
        grid=(rows // tm,),
        in_specs=[pl.BlockSpec((tm, D_MODEL), lambda i: (i, 0)),
                  pl.BlockSpec((tm, 2 * QK_ROPE_DIM), lambda i: (i % tiles, 0)),
                  pl.BlockSpec((2 * QK_ROPE_DIM, tm), lambda i: (0, i % tiles)),
                  _resident((D_MODEL, MLA_IN_COLS)), _resident((1, Q_LORA_RANK)), _resident((1, KV_LORA_RANK)),
                  _resident((N_HEADS * Q_GROUP, Q_LORA_RANK)),
                  _resident((KV_LORA_RANK, N_HEADS * QK_NOPE_DIM)),
                  _resident((N_HEADS * V_HEAD_DIM, KV_LORA_RANK))] + lat_specs + kr_specs,
        out_specs=[pl.BlockSpec((None, N_HEADS, QK_DIM, tm), lambda i: (i // tiles, 0, 0, i % tiles)),
                   pl.BlockSpec((None, N_HEADS, tm, QK_DIM), lambda i: (i // tiles, 0, i % tiles, 0)),
                   pl.BlockSpec((None, N_HEADS, None, V_ROWS, tm), lambda i: (i // tiles, 0, i % tiles, 0, 0)),
                   pl.BlockSpec((None, tm, KV_LORA_RANK), lambda i: (layer, i, 0)),
                   pl.BlockSpec((None, tm, QK_ROPE_DIM), lambda i: (layer, i, 0))],
        out_shape=[jax.ShapeDtypeStruct((batch, N_HEADS, QK_DIM, seq), BF16),
                   jax.ShapeDtypeStruct((batch, N_HEADS, seq, QK_DIM), BF16),
                   jax.ShapeDtypeStruct((batch, N_HEADS, tiles, V_ROWS, tm), BF16),
                   jax.ShapeDtypeStruct((DEPTH, rows, KV_LORA_RANK), F32),
                   jax.ShapeDtypeStruct((DEPTH, rows, QK_ROPE_DIM), F32)],
        input_output_aliases={**lat_alias, **kr_alias},
        compiler_params=_params("parallel"),
        name="mla_in",
    )(x, rope_table, rope_table_t, wa, qgain, kvgain, wuqt, wuk, wuvt, *lat_extra, *kr_extra)


def _kv_past_kernel(lat_ref, krope_ref, wuk_ref, wuvt_ref, kcat_ref, vt_ref):
    _store_kv_heads(lat_ref[...].astype(BF16), krope_ref[...].astype(BF16), wuk_ref, wuvt_ref, kcat_ref, vt_ref)


def _kv_past(lat, krope, wuk, wuvt, *, batch, seq):
    return pl.pallas_call(
        _kv_past_kernel,
        grid=(batch,),
        in_specs=[pl.BlockSpec((seq, KV_LORA_RANK), lambda b: (b, 0)),
                  pl.BlockSpec((seq, QK_ROPE_DIM), lambda b: (b, 0)),
                  _resident((KV_LORA_RANK, N_HEADS * QK_NOPE_DIM)),
                  _resident((N_HEADS * V_HEAD_DIM, KV_LORA_RANK))],
        out_specs=[pl.BlockSpec((None, N_HEADS, seq, QK_DIM), lambda b: (b, 0, 0, 0)),
                   pl.BlockSpec((None, N_HEADS, V_ROWS, seq), lambda b: (b, 0, 0, 0))],
        out_shape=[jax.ShapeDtypeStruct((batch, N_HEADS, seq, QK_DIM), BF16),
                   jax.ShapeDtypeStruct((batch, N_HEADS, V_ROWS, seq), BF16)],
        compiler_params=_params("parallel"),
        name="kv_past",
    )(lat, krope, wuk, wuvt)


def _conv_in_kernel(x_ref, state_ref, wbcu_ref, convw_ref, wco_ref, *rest, tm, tiles):
    yconv_ref, newstate_ref, ubuf_ref = rest[-3:]
    xb = x_ref[...].astype(BF16)
    bcu = _dot(xb, wbcu_ref[...])
    conv_b = bcu[:, 0:D_CONV]
    u = bcu[:, D_CONV:2 * D_CONV] * bcu[:, 2 * D_CONV:3 * D_CONV]

    @pl.when(pl.program_id(0) % tiles == 0)
    def _():
        ubuf_ref[CONV_HALO - 2:CONV_HALO, :] = state_ref[...]

    ubuf_ref[CONV_HALO:CONV_HALO + tm, :] = u
    u_prev1 = ubuf_ref[CONV_HALO - 1:CONV_HALO - 1 + tm, :]
    u_prev2 = ubuf_ref[CONV_HALO - 2:CONV_HALO - 2 + tm, :]
    conv = convw_ref[2:3, :] * u + convw_ref[0:1, :] * u_prev2 + convw_ref[1:2, :] * u_prev1
    tail = u[tm - 2:tm, :]
    ubuf_ref[CONV_HALO - 2:CONV_HALO, :] = tail
    newstate_ref[...] = tail

    yconv_ref[...] = _dot((conv_b * conv).astype(BF16), wco_ref[...])


def _conv_in(x, state, wbcu, convw, wco, state_stack, *, layer, batch, seq, tm):
    tiles = seq // tm
    rows = batch * seq
    row_spec = pl.BlockSpec((tm, D_MODEL), lambda i: (i, 0))
    extra, extra_specs, aliases = _stacked_out(layer, state_stack, 5, 1)
    return pl.pallas_call(
        functools.partial(_conv_in_kernel, tm=tm, tiles=tiles),
        grid=(rows // tm,),
        in_specs=[row_spec, pl.BlockSpec((None, CONV_WIDTH - 1, D_CONV), lambda i: (i // tiles, 0, 0)),
                  _resident((D_MODEL, 3 * D_CONV)), _resident((CONV_WIDTH, D_CONV)),
                  _resident((D_CONV, D_MODEL))] + extra_specs,
        out_specs=[row_spec,
                   pl.BlockSpec((None, None, CONV_WIDTH - 1, D_CONV), lambda i: (layer, i // tiles, 0, 0))],
        out_shape=[jax.ShapeDtypeStruct((rows, D_MODEL), F32),
                   jax.ShapeDtypeStruct((DEPTH, batch, CONV_WIDTH - 1, D_CONV), F32)],
        scratch_shapes=[pltpu.VMEM((CONV_HALO + tm, D_CONV), F32)],
        input_output_aliases=aliases,
        compiler_params=_params("arbitrary"),
        name="conv_in",
    )(x, state, wbcu, convw, wco, *extra)


def _chunk_mask_t(k_pos0, q_pos0, tk, tq):
    k_chunk = (k_pos0 + lax.broadcasted_iota(jnp.int32, (tk, tq), 0)) // CHUNK
    q_chunk = (q_pos0 + lax.broadcasted_iota(jnp.int32, (tk, tq), 1)) // CHUNK
    return k_chunk <= q_chunk


def _prompt_attn_kernel(qt_ref, k_ref, vt_ref, o_ref, *scratch, tile, group, split):
    s_refs, acc_ref = scratch[:group], scratch[group]
    qi = pl.program_id(2)
    mask = _chunk_mask_t(0, 0, tile, tile)
    per = group // split
    subsets = [tuple(range(n * per, (n + 1) * per)) for n in range(split)]

    def key_tile(j, col_max, score_heads, value_heads, value_max, masked):
        start = pl.multiple_of(j * tile, tile)
        out = []
        for g, m_run in zip(score_heads, col_max):
            s_t = _dot(k_ref[g, pl.ds(start, tile), :], qt_ref[g])
            if masked:
                s_t = jnp.where(mask, s_t, MASK_VALUE)
            s_refs[g][j] = s_t
            out.append(jnp.maximum(m_run, jnp.max(s_t, axis=0, keepdims=True)))
        for g, m_fin in zip(value_heads, value_max):
            p_t = jnp.exp2(s_refs[g][j] - m_fin).astype(BF16)
            acc_ref[g] += _dot(vt_ref[g, j], p_t)
        return tuple(out)

    value_heads, value_max = (), ()
    for score_heads in subsets + [()]:
        for g in value_heads:
            acc_ref[g] = jnp.zeros((V_ROWS, tile), F32)
        step = functools.partial(key_tile, score_heads=score_heads, value_heads=value_heads,
                                 value_max=value_max)
        visible = functools.partial(step, masked=False)
        col_max = lax.fori_loop(0, qi // 2, lambda u, c: visible(2 * u + 1, visible(2 * u, c)),
                                tuple(jnp.full((1, tile), MASK_VALUE, F32) for _ in score_heads))
        col_max = lax.cond(qi % 2 == 1, lambda c: visible(qi - 1, c), lambda c: c, col_max)
        col_max = step(qi, col_max, masked=True)
        for g in value_heads:
            o_t = acc_ref[g, 0:V_HEAD_DIM, :] * (1.0 / acc_ref[g, V_HEAD_DIM:V_HEAD_DIM + 1, :])
            o_ref[:, g * V_HEAD_DIM:(g + 1) * V_HEAD_DIM] = o_t.T.astype(o_ref.dtype)
        value_heads, value_max = score_heads, col_max


def _prompt_attn(q_t, k, v_t, *, tile, group, split):
    batch, heads, _, seq = q_t.shape
    tiles = seq // tile
    return pl.pallas_call(
        functools.partial(_prompt_attn_kernel, tile=tile, group=group, split=split),
        grid=(batch, heads // group, tiles),
        in_specs=[pl.BlockSpec((None, group, QK_DIM, tile), lambda b, h, i: (b, h, 0, i)),
                  pl.BlockSpec((None, group, seq, QK_DIM), lambda b, h, i: (b, h, 0, 0),
                               pipeline_mode=pl.Buffered(1)),
                  pl.BlockSpec((None, group, tiles, V_ROWS, tile), lambda b, h, i: (b, h, 0, 0, 0),
                               pipeline_mode=pl.Buffered(1))],
        out_specs=pl.BlockSpec((tile, group * V_HEAD_DIM), lambda b, h, i: (b * tiles + i, h)),
        out_shape=jax.ShapeDtypeStruct((batch * seq, heads * V_HEAD_DIM), BF16),
        scratch_shapes=[pltpu.VMEM((tiles, tile, tile), F32) for _ in range(group)]
                       + [pltpu.VMEM((group, V_ROWS, tile), F32)],
        compiler_params=_params("parallel", "parallel", "arbitrary"),
        name="prompt_attn",
    )(q_t, k, v_t)


def _sample_attn_kernel(qt_ref, kp_ref, vpt_ref, kn_ref, vnt_ref, o_ref, *, past, new):
    mask_past = _chunk_mask_t(0, past, past, new)
    mask_new = _chunk_mask_t(past, past, new, new)
    for h in range(N_HEADS):
        q_t = qt_ref[h]
        s_past = jnp.where(mask_past, _dot(kp_ref[h], q_t), MASK_VALUE)
        s_new = jnp.where(mask_new, _dot(kn_ref[h], q_t), MASK_VALUE)
        m = jnp.maximum(jnp.max(s_past, axis=0, keepdims=True), jnp.max(s_new, axis=0, keepdims=True))
        p_past = jnp.exp2(s_past - m).astype(BF16)
        p_new = jnp.exp2(s_new - m).astype(BF16)
        acc = _dot(vpt_ref[h], p_past) + _dot(vnt_ref[h, 0], p_new)
        o_ref[h] = (acc[0:V_HEAD_DIM, :] * (1.0 / acc[V_HEAD_DIM:V_HEAD_DIM + 1, :])).astype(o_ref.dtype)


def _sample_attn(q_t, k_past, v_past_t, k_new, v_new_t):
    batch, heads, _, new = q_t.shape
    past = k_past.shape[2]

    def spec(*dims):
        return pl.BlockSpec((None,) + dims, lambda b: (b,) + (0,) * len(dims))

    return pl.pallas_call(
        functools.partial(_sample_attn_kernel, past=past, new=new),
        grid=(batch,),
        in_specs=[spec(heads, QK_DIM, new), spec(heads, past, QK_DIM), spec(heads, V_ROWS, past),
                  spec(heads, new, QK_DIM), spec(heads, 1, V_ROWS, new)],
        out_specs=spec(heads, V_HEAD_DIM, new),
        out_shape=jax.ShapeDtypeStruct((batch, heads, V_HEAD_DIM, new), BF16),
        compiler_params=_params("parallel"),
        name="sample_attn",
    )(q_t, k_past, v_past_t, k_new, v_new_t)


def _mix_ln_kernel(o_ref, yconv_ref, x_ref, wg_ref, bgate_ref, wmo_ref, wmix_ref, gain_ref, bias_ref, out_ref):
    x = x_ref[...]
    gates = _dot(x.astype(BF16), wg_ref[...])
    gate_conv = jax.nn.sigmoid(gates[:, 0:D_MODEL] + bgate_ref[0:1, :])
    gate_mla = jax.nn.sigmoid(gates[:, D_MODEL:2 * D_MODEL] + bgate_ref[1:2, :])
    y_mla = _dot(o_ref[...], wmo_ref[...])
    merged = (gate_conv * yconv_ref[...] + gate_mla * y_mla).astype(BF16)
    mixed = _dot(merged, wmix_ref[...])
    out_ref[...] = _layer_norm(DEEPNORM_ALPHA * x + mixed, gain_ref[...], bias_ref[...])


def _mix_ln(o, y_conv, x, wg, bgate, wmo, wmix, gain, bias, *, tm):
    rows = x.shape[0]
    row_spec = pl.BlockSpec((tm, D_MODEL), lambda i: (i, 0))
    return pl.pallas_call(
        _mix_ln_kernel,
        grid=(rows // tm,),
        in_specs=[row_spec, row_spec, row_spec, _resident((D_MODEL, 2 * D_MODEL)), _resident((2, D_MODEL)),
                  _resident((N_HEADS * V_HEAD_DIM, D_MODEL)), _resident((D_MODEL, D_MODEL)),
                  _resident((1, D_MODEL)), _resident((1, D_MODEL))],
        out_specs=row_spec,
        out_shape=jax.ShapeDtypeStruct((rows, D_MODEL), F32),
        compiler_params=_params("parallel"),
        name="mix_ln",
    )(o, y_conv, x, wg, bgate, wmo, wmix, gain, bias)


def _rope_tables(pos0, length):
    half = QK_ROPE_DIM // 2
    inv = ROPE_BASE ** (-jnp.arange(half, dtype=F32) / half)
    ang = (pos0 + jnp.arange(length)).astype(F32)[:, None] * inv[None, :]
    cos, sin = jnp.cos(ang), jnp.sin(ang)
    table = jnp.concatenate([cos, cos, sin, sin], axis=-1)
    return table, table.T


def _rotate_half_cols(w):
    half = QK_ROPE_DIM // 2
    return jnp.concatenate([-w[..., half:], w[..., :half]], axis=-1)


def _prepare_weights(ffn1_w_gate_up, ffn1_w_down, ffn2_w_gate_up, ffn2_w_down, w_in, w_uq, w_ukv,
                     w_mla_out, w_conv_out, w_mix_out):
    mla_cols = Q_LORA_RANK + KV_LORA_RANK + QK_ROPE_DIM
    k_rope_cols = w_in[:, :, Q_LORA_RANK + KV_LORA_RANK:mla_cols]
    wa = jnp.concatenate([w_in[:, :, :mla_cols], _rotate_half_cols(k_rope_cols)], axis=-1)
    uq = w_uq.reshape(DEPTH, Q_LORA_RANK, N_HEADS, QK_DIM)
    uq = jnp.concatenate([uq, _rotate_half_cols(uq[..., QK_NOPE_DIM:])], axis=-1)
    uq_t = uq.reshape(DEPTH, Q_LORA_RANK, N_HEADS * Q_GROUP).transpose(0, 2, 1)
    ukv = w_ukv.reshape(DEPTH, KV_LORA_RANK, N_HEADS, QK_NOPE_DIM + V_HEAD_DIM)
    uk = ukv[..., :QK_NOPE_DIM].reshape(DEPTH, KV_LORA_RANK, N_HEADS * QK_NOPE_DIM)
    uv_t = ukv[..., QK_NOPE_DIM:].reshape(DEPTH, KV_LORA_RANK, N_HEADS * V_HEAD_DIM).transpose(0, 2, 1)
    return dict(
        ffn1_g=ffn1_w_gate_up[:, :, :D_FF].astype(BF16), ffn1_u=ffn1_w_gate_up[:, :, D_FF:].astype(BF16),
        ffn1_d=ffn1_w_down.astype(BF16),
        ffn2_g=ffn2_w_gate_up[:, :, :D_FF].astype(BF16), ffn2_u=ffn2_w_gate_up[:, :, D_FF:].astype(BF16),
        ffn2_d=ffn2_w_down.astype(BF16),
        wa=wa.astype(BF16),
        wbcu=w_in[:, :, mla_cols:mla_cols + 3 * D_CONV].astype(BF16),
        wgates=w_in[:, :, mla_cols + 3 * D_CONV:].astype(BF16),
        wuqt=uq_t.astype(BF16), wuk=uk.astype(BF16), wuvt=uv_t.astype(BF16),
        wmo=w_mla_out.astype(BF16), wco=w_conv_out.astype(BF16), wmix=w_mix_out.astype(BF16),
    )


def _run_trunk(x, cache_lat, cache_kr, conv_state, w, ln_gain, ln_bias, b_gate, q_norm_gain, kv_norm_gain,
               conv_w, *, tm, seq_tm, attn_group, attn_split):
    batch, seq, _ = x.shape
    past = 0 if cache_lat is None else cache_lat.shape[2]
    rope_table, rope_table_t = _rope_tables(past, seq)
    x = x.reshape(batch * seq, D_MODEL)
    lat = kr = conv_new = None
    for l in range(DEPTH):
        gain = lambda k: ln_gain[l, k][None, :]
        bias = lambda k: ln_bias[l, k][None, :]
        state = jnp.zeros((batch, CONV_WIDTH - 1, D_CONV), F32) if conv_state is None else conv_state[l]
        x = _ffn_ln(x, w["ffn1_g"][l], w["ffn1_u"][l], w["ffn1_d"][l], gain(0), bias(0), tm=tm)
        q_t, k_new, v_new_t, lat, kr = _mla_in(
            x, rope_table, rope_table_t, w["wa"][l], q_norm_gain[l][None, :], kv_norm_gain[l][None, :],
            w["wuqt"][l], w["wuk"][l], w["wuvt"][l], lat, kr, layer=l, batch=batch, seq=seq, tm=seq_tm)
        y_conv, conv_new = _conv_in(x, state, w["wbcu"][l], conv_w[l], w["wco"][l], conv_new,
                                    layer=l, batch=batch, seq=seq, tm=seq_tm)
        if cache_lat is None:
            o = _prompt_attn(q_t, k_new, v_new_t, tile=seq_tm, group=attn_group, split=attn_split[l])
        else:
            k_past, v_past_t = _kv_past(cache_lat[l].reshape(batch * past, KV_LORA_RANK),
                                        cache_kr[l].reshape(batch * past, QK_ROPE_DIM),
                                        w["wuk"][l], w["wuvt"][l], batch=batch, seq=past)
            o_t = _sample_attn(q_t, k_past, v_past_t, k_new, v_new_t)
            o = o_t.transpose(0, 3, 1, 2).reshape(batch * seq, N_HEADS * V_HEAD_DIM)
        x = _mix_ln(o, y_conv, x, w["wgates"][l], b_gate[l], w["wmo"][l], w["wmix"][l], gain(1), bias(1), tm=tm)
        x = _ffn_ln(x, w["ffn2_g"][l], w["ffn2_u"][l], w["ffn2_d"][l], gain(2), bias(2), tm=tm)
    return (x.reshape(batch, seq, D_MODEL), lat.reshape(DEPTH, batch, seq, KV_LORA_RANK),
            kr.reshape(DEPTH, batch, seq, QK_ROPE_DIM), conv_new)


def kernel(x_prompt, x_sample, cache_kv_latent, cache_k_rope, state_conv, ffn1_w_gate_up, ffn1_w_down,
           ffn2_w_gate_up, ffn2_w_down, ln_gain, ln_bias, w_in, b_gate, q_norm_gain, kv_norm_gain, w_uq, w_ukv,
           w_mla_out, conv_w, w_conv_out, w_mix_out):
    w = _prepare_weights(ffn1_w_gate_up, ffn1_w_down, ffn2_w_gate_up, ffn2_w_down, w_in, w_uq, w_ukv,
                         w_mla_out, w_conv_out, w_mix_out)
    shared = (w, ln_gain, ln_bias, b_gate, q_norm_gain, kv_norm_gain, conv_w)
    y_prompt, p_lat, p_kr, p_conv = _run_trunk(x_prompt, None, None, None, *shared,
                                               tm=512, seq_tm=512, attn_group=4, attn_split=(1, 1, 2, 1))
    y_sample, s_lat, s_kr, s_conv = _run_trunk(x_sample, cache_kv_latent, cache_k_rope, state_conv, *shared,
                                               tm=512, seq_tm=x_sample.shape[1], attn_group=None, attn_split=None)
    return (y_prompt, y_sample, p_lat, p_kr, p_conv, s_lat, s_kr, s_conv)
```

```python
import functools
import math

import jax
import jax.numpy as jnp
from jax import lax
from jax.experimental import pallas as pl
from jax.experimental.pallas import tpu as pltpu

D_MODEL = 1024
DEPTH = 4
CHUNK = 64
N_HEADS = 8
QK_NOPE_DIM = 128
QK_ROPE_DIM = 64
V_HEAD_DIM = 128
Q_LORA_RANK = 384
KV_LORA_RANK = 256
ROPE_BASE = 10000.0
D_CONV = D_MODEL
CONV_WIDTH = 3
D_FF = 2816
NORM_EPS = 1e-5
QK_DIM = QK_NOPE_DIM + QK_ROPE_DIM
ATTN_SCALE = QK_DIM ** -0.5
EXP2_SCALE = ATTN_SCALE * math.log2(math.e)
DEEPNORM_ALPHA = (2 * DEPTH) ** 0.25
MASK_VALUE = -1e30

Q_GROUP = QK_NOPE_DIM + 2 * QK_ROPE_DIM
MLA_IN_COLS = Q_LORA_RANK + KV_LORA_RANK + 2 * QK_ROPE_DIM
CONV_HALO = 8
BF16_SUBLANES = 16
V_ROWS = V_HEAD_DIM + BF16_SUBLANES

VMEM_LIMIT_BYTES = 56 * 1024 * 1024

F32 = jnp.float32
BF16 = jnp.bfloat16


def _dot(a, b):
    return jnp.dot(a, b, preferred_element_type=F32)


def _dot_nt(a, b):
    return lax.dot_general(a, b, (((1,), (1,)), ((), ())), preferred_element_type=F32)


def _layer_norm(z, gain, bias):
    mu = jnp.mean(z, axis=-1, keepdims=True)
    zc = z - mu
    var = jnp.mean(zc * zc, axis=-1, keepdims=True)
    return zc * lax.rsqrt(var + NORM_EPS) * gain + bias


def _rms_norm(z, gain):
    ms = jnp.mean(z * z, axis=-1, keepdims=True)
    return z * lax.rsqrt(ms + NORM_EPS) * gain


def _resident(shape):
    zeros = (0,) * len(shape)
    return pl.BlockSpec(shape, lambda *_: zeros, pipeline_mode=pl.Buffered(1))


def _stacked_out(layer, stacked, in_count, out_index):
    if layer == 0:
        return [], [], {}
    return [stacked], [pl.BlockSpec(memory_space=pl.ANY)], {in_count: out_index}


def _params(*semantics):
    return pltpu.CompilerParams(dimension_semantics=semantics, vmem_limit_bytes=VMEM_LIMIT_BYTES)


def _ffn_ln_kernel(x_ref, wg_ref, wu_ref, wd_ref, gain_ref, bias_ref, o_ref, *, parts):
    rows = x_ref.shape[0] // parts
    for part in range(parts):
        sl = slice(part * rows, (part + 1) * rows)
        x = x_ref[sl, :]
        xb = x.astype(BF16)
        gate = _dot(xb, wg_ref[...])
        up = _dot(xb, wu_ref[...])
        hidden = (gate * jax.nn.sigmoid(gate) * up).astype(BF16)
        y = _dot(hidden, wd_ref[...])
        o_ref[sl, :] = _layer_norm(DEEPNORM_ALPHA * x + 0.5 * y, gain_ref[...], bias_ref[...])


def _ffn_ln(x, wg, wu, wd, gain, bias, *, tm, parts=2):
    rows = x.shape[0]
    row_spec = pl.BlockSpec((tm, D_MODEL), lambda i: (i, 0))
    return pl.pallas_call(
        functools.partial(_ffn_ln_kernel, parts=parts),
        grid=(rows // tm,),
        in_specs=[row_spec, _resident((D_MODEL, D_FF)), _resident((D_MODEL, D_FF)),
                  _resident((D_FF, D_MODEL)), _resident((1, D_MODEL)), _resident((1, D_MODEL))],
        out_specs=row_spec,
        out_shape=jax.ShapeDtypeStruct((rows, D_MODEL), F32),
        compiler_params=_params("parallel"),
        name="ffn_ln",
    )(x, wg, wu, wd, gain, bias)


def _store_kv_heads(lat_bf16, k_rope_bf16, wuk_ref, wuvt_ref, kcat_ref, vt_ref):
    k_nope = _dot(lat_bf16, wuk_ref[...])
    v_t = _dot_nt(wuvt_ref[...], lat_bf16)
    ones = jnp.ones((V_ROWS - V_HEAD_DIM, lat_bf16.shape[0]), BF16)
    for h in range(N_HEADS):
        kcat_ref[h, :, 0:QK_NOPE_DIM] = k_nope[:, h * QK_NOPE_DIM:(h + 1) * QK_NOPE_DIM].astype(BF16)
        kcat_ref[h, :, QK_NOPE_DIM:QK_DIM] = k_rope_bf16
        vt_ref[h, 0:V_HEAD_DIM, :] = v_t[h * V_HEAD_DIM:(h + 1) * V_HEAD_DIM, :].astype(BF16)
        vt_ref[h, V_HEAD_DIM:V_ROWS, :] = ones


def _mla_in_kernel(x_ref, rope_ref, ropet_ref, wa_ref, qgain_ref, kvgain_ref, wuqt_ref, wuk_ref, wuvt_ref,
                   *rest):
    qt_ref, kcat_ref, vt_ref, lat_ref, krope_ref = rest[-5:]
    xb = x_ref[...].astype(BF16)
    h = _dot(xb, wa_ref[...])
    q_lat = h[:, 0:Q_LORA_RANK]
    kv_lat = h[:, Q_LORA_RANK:Q_LORA_RANK + KV_LORA_RANK]
    k_pair = h[:, Q_LORA_RANK + KV_LORA_RANK:MLA_IN_COLS]

    w = k_pair * rope_ref[...]
    k_rope = (w + pltpu.roll(w, QK_ROPE_DIM, axis=1))[:, 0:QK_ROPE_DIM]
    krope_ref[...] = k_rope
    c_kv = _rms_norm(kv_lat, kvgain_ref[...])
    lat_ref[...] = c_kv
    _store_kv_heads(c_kv.astype(BF16), k_rope.astype(BF16), wuk_ref, wuvt_ref, kcat_ref, vt_ref)

    qn = _rms_norm(q_lat, qgain_ref[...]).astype(BF16)
    q_t = _dot_nt(wuqt_ref[...], qn) * EXP2_SCALE
    cos_t = ropet_ref[0:QK_ROPE_DIM, :]
    sin_t = ropet_ref[QK_ROPE_DIM:2 * QK_ROPE_DIM, :]
    for hd in range(N_HEADS):
        base = hd * Q_GROUP
        rope0 = base + QK_NOPE_DIM
        qt_ref[hd, 0:QK_NOPE_DIM, :] = q_t[base:rope0, :].astype(BF16)
        roped = q_t[rope0:rope0 + QK_ROPE_DIM, :] * cos_t + q_t[rope0 + QK_ROPE_DIM:base + Q_GROUP, :] * sin_t
        qt_ref[hd, QK_NOPE_DIM:QK_DIM, :] = roped.astype(BF16)


def _mla_in(x, rope_table, rope_table_t, wa, qgain, kvgain, wuqt, wuk, wuvt, lat_stack, krope_stack, *,
            layer, batch, seq, tm):
    tiles = seq // tm
    rows = batch * seq
    lat_extra, lat_specs, lat_alias = _stacked_out(layer, lat_stack, 9, 3)
    kr_extra, kr_specs, kr_alias = _stacked_out(layer, krope_stack, 10, 4)
    return pl.pallas_call(
        _mla_in_kernel,
        grid=(rows // tm,),
        in_specs=[pl.BlockSpec((tm, D_MODEL), lambda i: (i, 0)),
                  pl.BlockSpec((tm, 2 * QK_ROPE_DIM), lambda i: (i % tiles, 0)),
                  pl.BlockSpec((2 * QK_ROPE_DIM, tm), lambda i: (0, i % tiles)),
                  _resident((D_MODEL, MLA_IN_COLS)), _resident((1, Q_LORA_RANK)), _resident((1, KV_LORA_RANK)),
                  _resident((N_HEADS * Q_GROUP, Q_LORA_RANK)),
                  _resident((KV_LORA_RANK, N_HEADS * QK_NOPE_DIM)),
                  _resident((N_HEADS * V_HEAD_DIM, KV_LORA_RANK))] + lat_specs + kr_specs,
        out_specs=[pl.BlockSpec((None, N_HEADS, QK_DIM, tm), lambda i: (i // tiles, 0, 0, i % tiles)),
                   pl.BlockSpec((None, N_HEADS, tm, QK_DIM), lambda i: (i // tiles, 0, i % tiles, 0)),
                   pl.BlockSpec((None, N_HEADS, None, V_ROWS, tm), lambda i: (i // tiles, 0, i % tiles, 0, 0)),
                   pl.BlockSpec((None, tm, KV_LORA_RANK), lambda i: (layer, i, 0)),
                   pl.BlockSpec((None, tm, QK_ROPE_DIM), lambda i: (layer, i, 0))],
        out_shape=[jax.ShapeDtypeStruct((batch, N_HEADS, QK_DIM, seq), BF16),
                   jax.ShapeDtypeStruct((batch, N_HEADS, seq, QK_DIM), BF16),
                   jax.ShapeDtypeStruct((batch, N_HEADS, tiles, V_ROWS, tm), BF16),
                   jax.ShapeDtypeStruct((DEPTH, rows, KV_LORA_RANK), F32),
                   jax.ShapeDtypeStruct((DEPTH, rows, QK_ROPE_DIM), F32)],
        input_output_aliases={**lat_alias, **kr_alias},
        compiler_params=_params("parallel"),
        name="mla_in",
    )(x, rope_table, rope_table_t, wa, qgain, kvgain, wuqt, wuk, wuvt, *lat_extra, *kr_extra)


def _kv_past_kernel(lat_ref, krope_ref, wuk_ref, wuvt_ref, kcat_ref, vt_ref):
    _store_kv_heads(lat_ref[...].astype(BF16), krope_ref[...].astype(BF16), wuk_ref, wuvt_ref, kcat_ref, vt_ref)


def _kv_past(lat, krope, wuk, wuvt, *, batch, seq):
    return pl.pallas_call(
        _kv_past_kernel,
        grid=(batch,),
        in_specs=[pl.BlockSpec((seq, KV_LORA_RANK), lambda b: (b, 0)),
                  pl.BlockSpec((seq, QK_ROPE_DIM), lambda b: (b, 0)),
                  _resident((KV_LORA_RANK, N_HEADS * QK_NOPE_DIM)),
                  _resident((N_HEADS * V_HEAD_DIM, KV_LORA_RANK))],
        out_specs=[pl.BlockSpec((None, N_HEADS, seq, QK_DIM), lambda b: (b, 0, 0, 0)),
                   pl.BlockSpec((None, N_HEADS, V_ROWS, seq), lambda b: (b, 0, 0, 0))],
        out_shape=[jax.ShapeDtypeStruct((batch, N_HEADS, seq, QK_DIM), BF16),
                   jax.ShapeDtypeStruct((batch, N_HEADS, V_ROWS, seq), BF16)],
        compiler_params=_params("parallel"),
        name="kv_past",
    )(lat, krope, wuk, wuvt)


def _conv_in_kernel(x_ref, state_ref, wbcu_ref, convw_ref, wco_ref, *rest, tm, tiles, parts):
    yconv_ref, newstate_ref, ubuf_ref = rest[-3:]

    @pl.when(pl.program_id(0) % tiles == 0)
    def _():
        ubuf_ref[CONV_HALO - 2:CONV_HALO, :] = state_ref[...]

    rows = tm // parts
    for part in range(parts):
        r0 = part * rows
        xb = x_ref[r0:r0 + rows, :].astype(BF16)
        bcu = _dot(xb, wbcu_ref[...])
        conv_b = bcu[:, 0:D_CONV]
        u = bcu[:, D_CONV:2 * D_CONV] * bcu[:, 2 * D_CONV:3 * D_CONV]
        ubuf_ref[CONV_HALO + r0:CONV_HALO + r0 + rows, :] = u
        u_prev1 = ubuf_ref[CONV_HALO - 1 + r0:CONV_HALO - 1 + r0 + rows, :]
        u_prev2 = ubuf_ref[CONV_HALO - 2 + r0:CONV_HALO - 2 + r0 + rows, :]
        conv = convw_ref[2:3, :] * u + convw_ref[0:1, :] * u_prev2 + convw_ref[1:2, :] * u_prev1
        yconv_ref[r0:r0 + rows, :] = _dot((conv_b * conv).astype(BF16), wco_ref[...])
    tail = ubuf_ref[CONV_HALO + tm - 2:CONV_HALO + tm, :]
    ubuf_ref[CONV_HALO - 2:CONV_HALO, :] = tail
    newstate_ref[...] = tail


def _conv_in(x, state, wbcu, convw, wco, state_stack, *, layer, batch, seq, tm):
    tiles = seq // tm
    rows = batch * seq
    row_spec = pl.BlockSpec((tm, D_MODEL), lambda i: (i, 0))
    extra, extra_specs, aliases = _stacked_out(layer, state_stack, 5, 1)
    return pl.pallas_call(
        functools.partial(_conv_in_kernel, tm=tm, tiles=tiles, parts=2),
        grid=(rows // tm,),
        in_specs=[row_spec, pl.BlockSpec((None, CONV_WIDTH - 1, D_CONV), lambda i: (i // tiles, 0, 0)),
                  _resident((D_MODEL, 3 * D_CONV)), _resident((CONV_WIDTH, D_CONV)),
                  _resident((D_CONV, D_MODEL))] + extra_specs,
        out_specs=[row_spec,
                   pl.BlockSpec((None, None, CONV_WIDTH - 1, D_CONV), lambda i: (layer, i // tiles, 0, 0))],
        out_shape=[jax.ShapeDtypeStruct((rows, D_MODEL), F32),
                   jax.ShapeDtypeStruct((DEPTH, batch, CONV_WIDTH - 1, D_CONV), F32)],
        scratch_shapes=[pltpu.VMEM((CONV_HALO + tm, D_CONV), F32)],
        input_output_aliases=aliases,
        compiler_params=_params("arbitrary"),
        name="conv_in",
    )(x, state, wbcu, convw, wco, *extra)


def _chunk_mask_t(k_pos0, q_pos0, tk, tq):
    k_chunk = (k_pos0 + lax.broadcasted_iota(jnp.int32, (tk, tq), 0)) // CHUNK
    q_chunk = (q_pos0 + lax.broadcasted_iota(jnp.int32, (tk, tq), 1)) // CHUNK
    return k_chunk <= q_chunk


def _prompt_attn_kernel(qt_ref, k_ref, vt_ref, o_ref, *scratch, tile, group, split):
    s_refs, acc_ref = scratch[:group], scratch[group]
    qi = pl.program_id(2)
    mask = _chunk_mask_t(0, 0, tile, tile)
    per = group // split
    subsets = [tuple(range(n * per, (n + 1) * per)) for n in range(split)]

    def key_tile(j, col_max, score_heads, value_heads, value_max, masked):
        start = pl.multiple_of(j * tile, tile)
        out = []
        for g, m_run in zip(score_heads, col_max):
            s_t = _dot(k_ref[g, pl.ds(start, tile), :], qt_ref[g])
            if masked:
                s_t = jnp.where(mask, s_t, MASK_VALUE)
            s_refs[g][j] = s_t
            out.append(jnp.maximum(m_run, jnp.max(s_t, axis=0, keepdims=True)))
        for g, m_fin in zip(value_heads, value_max):
            p_t = jnp.exp2(s_refs[g][j] - m_fin).astype(BF16)
            acc_ref[g] += _dot(vt_ref[g, j], p_t)
        return tuple(out)

    value_heads, value_max = (), ()
    for score_heads in subsets + [()]:
        for g in value_heads:
            acc_ref[g] = jnp.zeros((V_ROWS, tile), F32)
        step = functools.partial(key_tile, score_heads=score_heads, value_heads=value_heads,
                                 value_max=value_max)
        visible = functools.partial(step, masked=False)
        col_max = lax.fori_loop(0, qi // 2, lambda u, c: visible(2 * u + 1, visible(2 * u, c)),
                                tuple(jnp.full((1, tile), MASK_VALUE, F32) for _ in score_heads))
        col_max = lax.cond(qi % 2 == 1, lambda c: visible(qi - 1, c), lambda c: c, col_max)
        col_max = step(qi, col_max, masked=True)
        for g in value_heads:
            o_t = acc_ref[g, 0:V_HEAD_DIM, :] * (1.0 / acc_ref[g, V_HEAD_DIM:V_HEAD_DIM + 1, :])
            o_ref[:, g * V_HEAD_DIM:(g + 1) * V_HEAD_DIM] = o_t.T.astype(o_ref.dtype)
        value_heads, value_max = score_heads, col_max


def _prompt_attn(q_t, k, v_t, *, tile, group, split):
    batch, heads, _, seq = q_t.shape
    tiles = seq // tile
    return pl.pallas_call(
        functools.partial(_prompt_attn_kernel, tile=tile, group=group, split=split),
        grid=(batch, heads // group, tiles),
        in_specs=[pl.BlockSpec((None, group, QK_DIM, tile), lambda b, h, i: (b, h, 0, i)),
                  pl.BlockSpec((None, group, seq, QK_DIM), lambda b, h, i: (b, h, 0, 0),
                               pipeline_mode=pl.Buffered(1)),
                  pl.BlockSpec((None, group, tiles, V_ROWS, tile), lambda b, h, i: (b, h, 0, 0, 0),
                               pipeline_mode=pl.Buffered(1))],
        out_specs=pl.BlockSpec((tile, group * V_HEAD_DIM), lambda b, h, i: (b * tiles + i, h)),
        out_shape=jax.ShapeDtypeStruct((batch * seq, heads * V_HEAD_DIM), BF16),
        scratch_shapes=[pltpu.VMEM((tiles, tile, tile), F32) for _ in range(group)]
                       + [pltpu.VMEM((group, V_ROWS, tile), F32)],
        compiler_params=_params("parallel", "parallel", "arbitrary"),
        name="prompt_attn",
    )(q_t, k, v_t)


def _sample_attn_kernel(qt_ref, kp_ref, vpt_ref, kn_ref, vnt_ref, o_ref, *, past, new):
    mask_past = _chunk_mask_t(0, past, past, new)
    mask_new = _chunk_mask_t(past, past, new, new)
    for h in range(N_HEADS):
        q_t = qt_ref[h]
        s_past = jnp.where(mask_past, _dot(kp_ref[h], q_t), MASK_VALUE)
        s_new = jnp.where(mask_new, _dot(kn_ref[h], q_t), MASK_VALUE)
        m = jnp.maximum(jnp.max(s_past, axis=0, keepdims=True), jnp.max(s_new, axis=0, keepdims=True))
        p_past = jnp.exp2(s_past - m).astype(BF16)
        p_new = jnp.exp2(s_new - m).astype(BF16)
        acc = _dot(vpt_ref[h], p_past) + _dot(vnt_ref[h, 0], p_new)
        o_ref[h] = (acc[0:V_HEAD_DIM, :] * (1.0 / acc[V_HEAD_DIM:V_HEAD_DIM + 1, :])).astype(o_ref.dtype)


def _sample_attn(q_t, k_past, v_past_t, k_new, v_new_t):
    batch, heads, _, new = q_t.shape
    past = k_past.shape[2]

    def spec(*dims):
        return pl.BlockSpec((None,) + dims, lambda b: (b,) + (0,) * len(dims))

    return pl.pallas_call(
        functools.partial(_sample_attn_kernel, past=past, new=new),
        grid=(batch,),
        in_specs=[spec(heads, QK_DIM, new), spec(heads, past, QK_DIM), spec(heads, V_ROWS, past),
                  spec(heads, new, QK_DIM), spec(heads, 1, V_ROWS, new)],
        out_specs=spec(heads, V_HEAD_DIM, new),
        out_shape=jax.ShapeDtypeStruct((batch, heads, V_HEAD_DIM, new), BF16),
        compiler_params=_params("parallel"),
        name="sample_attn",
    )(q_t, k_past, v_past_t, k_new, v_new_t)


def _mix_ln_kernel(o_ref, yconv_ref, x_ref, wg_ref, bgate_ref, wmo_ref, wmix_ref, gain_ref, bias_ref, out_ref,
                   *, parts):
    rows = x_ref.shape[0] // parts
    for part in range(parts):
        sl = slice(part * rows, (part + 1) * rows)
        x = x_ref[sl, :]
        gates = _dot(x.astype(BF16), wg_ref[...])
        gate_conv = jax.nn.sigmoid(gates[:, 0:D_MODEL] + bgate_ref[0:1, :])
        gate_mla = jax.nn.sigmoid(gates[:, D_MODEL:2 * D_MODEL] + bgate_ref[1:2, :])
        y_mla = _dot(o_ref[sl, :], wmo_ref[...])
        merged = (gate_conv * yconv_ref[sl, :] + gate_mla * y_mla).astype(BF16)
        mixed = _dot(merged, wmix_ref[...])
        out_ref[sl, :] = _layer_norm(DEEPNORM_ALPHA * x + mixed, gain_ref[...], bias_ref[...])


def _mix_ln(o, y_conv, x, wg, bgate, wmo, wmix, gain, bias, *, tm, parts=2):
    rows = x.shape[0]
    row_spec = pl.BlockSpec((tm, D_MODEL), lambda i: (i, 0))
    return pl.pallas_call(
        functools.partial(_mix_ln_kernel, parts=parts),
        grid=(rows // tm,),
        in_specs=[row_spec, row_spec, row_spec, _resident((D_MODEL, 2 * D_MODEL)), _resident((2, D_MODEL)),
                  _resident((N_HEADS * V_HEAD_DIM, D_MODEL)), _resident((D_MODEL, D_MODEL)),
                  _resident((1, D_MODEL)), _resident((1, D_MODEL))],
        out_specs=row_spec,
        out_shape=jax.ShapeDtypeStruct((rows, D_MODEL), F32),
        compiler_params=_params("parallel"),
        name="mix_ln",
    )(o, y_conv, x, wg, bgate, wmo, wmix, gain, bias)


def _rope_tables(pos0, length):
    half = QK_ROPE_DIM // 2
    inv = ROPE_BASE ** (-jnp.arange(half, dtype=F32) / half)
    ang = (pos0 + jnp.arange(length)).astype(F32)[:, None] * inv[None, :]
    cos, sin = jnp.cos(ang), jnp.sin(ang)
    table = jnp.concatenate([cos, cos, sin, sin], axis=-1)
    return table, table.T


def _rotate_half_cols(w):
    half = QK_ROPE_DIM // 2
    return jnp.concatenate([-w[..., half:], w[..., :half]], axis=-1)


def _prepare_weights(ffn1_w_gate_up, ffn1_w_down, ffn2_w_gate_up, ffn2_w_down, w_in, w_uq, w_ukv,
                     w_mla_out, w_conv_out, w_mix_out):
    mla_cols = Q_LORA_RANK + KV_LORA_RANK + QK_ROPE_DIM
    k_rope_cols = w_in[:, :, Q_LORA_RANK + KV_LORA_RANK:mla_cols]
    wa = jnp.concatenate([w_in[:, :, :mla_cols], _rotate_half_cols(k_rope_cols)], axis=-1)
    uq = w_uq.reshape(DEPTH, Q_LORA_RANK, N_HEADS, QK_DIM)
    uq = jnp.concatenate([uq, _rotate_half_cols(uq[..., QK_NOPE_DIM:])], axis=-1)
    uq_t = uq.reshape(DEPTH, Q_LORA_RANK, N_HEADS * Q_GROUP).transpose(0, 2, 1)
    ukv = w_ukv.reshape(DEPTH, KV_LORA_RANK, N_HEADS, QK_NOPE_DIM + V_HEAD_DIM)
    uk = ukv[..., :QK_NOPE_DIM].reshape(DEPTH, KV_LORA_RANK, N_HEADS * QK_NOPE_DIM)
    uv_t = ukv[..., QK_NOPE_DIM:].reshape(DEPTH, KV_LORA_RANK, N_HEADS * V_HEAD_DIM).transpose(0, 2, 1)
    return dict(
        ffn1_g=ffn1_w_gate_up[:, :, :D_FF].astype(BF16), ffn1_u=ffn1_w_gate_up[:, :, D_FF:].astype(BF16),
        ffn1_d=ffn1_w_down.astype(BF16),
        ffn2_g=ffn2_w_gate_up[:, :, :D_FF].astype(BF16), ffn2_u=ffn2_w_gate_up[:, :, D_FF:].astype(BF16),
        ffn2_d=ffn2_w_down.astype(BF16),
        wa=wa.astype(BF16),
        wbcu=w_in[:, :, mla_cols:mla_cols + 3 * D_CONV].astype(BF16),
        wgates=w_in[:, :, mla_cols + 3 * D_CONV:].astype(BF16),
        wuqt=uq_t.astype(BF16), wuk=uk.astype(BF16), wuvt=uv_t.astype(BF16),
        wmo=w_mla_out.astype(BF16), wco=w_conv_out.astype(BF16), wmix=w_mix_out.astype(BF16),
    )


def _run_trunk(x, cache_lat, cache_kr, conv_state, w, ln_gain, ln_bias, b_gate, q_norm_gain, kv_norm_gain,
               conv_w, *, tm, ffn_tm, ffn_parts, seq_tm, attn_group, attn_split):
    batch, seq, _ = x.shape
    past = 0 if cache_lat is None else cache_lat.shape[2]
    rope_table, rope_table_t = _rope_tables(past, seq)
    x = x.reshape(batch * seq, D_MODEL)
    lat = kr = conv_new = None
    for l in range(DEPTH):
        gain = lambda k: ln_gain[l, k][None, :]
        bias = lambda k: ln_bias[l, k][None, :]
        state = jnp.zeros((batch, CONV_WIDTH - 1, D_CONV), F32) if conv_state is None else conv_state[l]
        x = _ffn_ln(x, w["ffn1_g"][l], w["ffn1_u"][l], w["ffn1_d"][l], gain(0), bias(0), tm=ffn_tm,
                    parts=ffn_parts)
        q_t, k_new, v_new_t, lat, kr = _mla_in(
            x, rope_table, rope_table_t, w["wa"][l], q_norm_gain[l][None, :], kv_norm_gain[l][None, :],
            w["wuqt"][l], w["wuk"][l], w["wuvt"][l], lat, kr, layer=l, batch=batch, seq=seq, tm=seq_tm)
        y_conv, conv_new = _conv_in(x, state, w["wbcu"][l], conv_w[l], w["wco"][l], conv_new,
                                    layer=l, batch=batch, seq=seq, tm=seq_tm)
        if cache_lat is None:
            o = _prompt_attn(q_t, k_new, v_new_t, tile=seq_tm, group=attn_group, split=attn_split[l])
        else:
            k_past, v_past_t = _kv_past(cache_lat[l].reshape(batch * past, KV_LORA_RANK),
                                        cache_kr[l].reshape(batch * past, QK_ROPE_DIM),
                                        w["wuk"][l], w["wuvt"][l], batch=batch, seq=past)
            o_t = _sample_attn(q_t, k_past, v_past_t, k_new, v_new_t)
            o = o_t.transpose(0, 3, 1, 2).reshape(batch * seq, N_HEADS * V_HEAD_DIM)
        x = _mix_ln(o, y_conv, x, w["wgates"][l], b_gate[l], w["wmo"][l], w["wmix"][l], gain(1), bias(1), tm=tm)
        x = _ffn_ln(x, w["ffn2_g"][l], w["ffn2_u"][l], w["ffn2_d"][l], gain(2), bias(2), tm=ffn_tm,
                    parts=ffn_parts)
    return (x.reshape(batch, seq, D_MODEL), lat.reshape(DEPTH, batch, seq, KV_LORA_RANK),
            kr.reshape(DEPTH, batch, seq, QK_ROPE_DIM), conv_new)


def kernel(x_prompt, x_sample, cache_kv_latent, cache_k_rope, state_conv, ffn1_w_gate_up, ffn1_w_down,
           ffn2_w_gate_up, ffn2_w_down, ln_gain, ln_bias, w_in, b_gate, q_norm_gain, kv_norm_gain, w_uq, w_ukv,
           w_mla_out, conv_w, w_conv_out, w_mix_out):
    w = _prepare_weights(ffn1_w_gate_up, ffn1_w_down, ffn2_w_gate_up, ffn2_w_down, w_in, w_uq, w_ukv,
                         w_mla_out, w_conv_out, w_mix_out)
    shared = (w, ln_gain, ln_bias, b_gate, q_norm_gain, kv_norm_gain, conv_w)
    y_prompt, p_lat, p_kr, p_conv = _run_trunk(x_prompt, None, None, None, *shared,
                                               tm=512, ffn_tm=1024, ffn_parts=4, seq_tm=512, attn_group=4, attn_split=(1, 1, 1, 1))
    y_sample, s_lat, s_kr, s_conv = _run_trunk(x_sample, cache_kv_latent, cache_k_rope, state_conv, *shared,
                                               tm=512, ffn_tm=512, ffn_parts=2, seq_tm=x_sample.shape[1], attn_group=None, attn_split=None)
    return (y_prompt, y_sample, p_lat, p_kr, p_conv, s_lat, s_kr, s_conv)
```

```python
import functools
import math

import jax
import jax.numpy as jnp
from jax import lax
from jax.experimental import pallas as pl
from jax.experimental.pallas import tpu as pltpu

D_MODEL = 1024
DEPTH = 4
CHUNK = 64
N_HEADS = 8
QK_NOPE_DIM = 128
QK_ROPE_DIM = 64
V_HEAD_DIM = 128
Q_LORA_RANK = 384
KV_LORA_RANK = 256
ROPE_BASE = 10000.0
D_CONV = D_MODEL
CONV_WIDTH = 3
D_FF = 2816
NORM_EPS = 1e-5
QK_DIM = QK_NOPE_DIM + QK_ROPE_DIM
ATTN_SCALE = QK_DIM ** -0.5
EXP2_SCALE = ATTN_SCALE * math.log2(math.e)
DEEPNORM_ALPHA = (2 * DEPTH) ** 0.25
MASK_VALUE = -1e30

Q_GROUP = QK_NOPE_DIM + 2 * QK_ROPE_DIM
MLA_IN_COLS = Q_LORA_RANK + KV_LORA_RANK + 2 * QK_ROPE_DIM
CONV_HALO = 8
BF16_SUBLANES = 16
V_ROWS = V_HEAD_DIM + BF16_SUBLANES

VMEM_LIMIT_BYTES = 56 * 1024 * 1024

F32 = jnp.float32
BF16 = jnp.bfloat16


def _dot(a, b):
    return jnp.dot(a, b, preferred_element_type=F32)


def _dot_nt(a, b):
    return lax.dot_general(a, b, (((1,), (1,)), ((), ())), preferred_element_type=F32)


def _layer_norm(z, gain, bias):
    mu = jnp.mean(z, axis=-1, keepdims=True)
    zc = z - mu
    var = jnp.mean(zc * zc, axis=-1, keepdims=True)
    return zc * lax.rsqrt(var + NORM_EPS) * gain + bias


def _rms_norm(z, gain):
    ms = jnp.mean(z * z, axis=-1, keepdims=True)
    return z * lax.rsqrt(ms + NORM_EPS) * gain


def _resident(shape, *lead):
    index = tuple(lead) + (0,) * len(shape)
    return pl.BlockSpec((None,) * len(lead) + tuple(shape), lambda *_: index, pipeline_mode=pl.Buffered(1))


STACK_SPEC = pl.BlockSpec(memory_space=pl.ANY)


def _params(*semantics):
    return pltpu.CompilerParams(dimension_semantics=semantics, vmem_limit_bytes=VMEM_LIMIT_BYTES)


def _ffn_ln_kernel(x_ref, wg_ref, wu_ref, wd_ref, gain_ref, bias_ref, o_ref, *, parts):
    rows = x_ref.shape[0] // parts
    for part in range(parts):
        sl = slice(part * rows, (part + 1) * rows)
        x = x_ref[sl, :]
        xb = x.astype(BF16)
        gate = _dot(xb, wg_ref[...])
        up = _dot(xb, wu_ref[...])
        hidden = (gate * jax.nn.sigmoid(gate) * up).astype(BF16)
        y = _dot(hidden, wd_ref[...])
        o_ref[sl, :] = _layer_norm(DEEPNORM_ALPHA * x + 0.5 * y, gain_ref[...], bias_ref[...])


def _ffn_ln(x, wg, wu, wd, gain, bias, *, layer, tm, parts):
    rows = x.shape[0]
    row_spec = pl.BlockSpec((tm, D_MODEL), lambda i: (i, 0))
    return pl.pallas_call(
        functools.partial(_ffn_ln_kernel, parts=parts),
        grid=(rows // tm,),
        in_specs=[row_spec, _resident((D_MODEL, D_FF), layer), _resident((D_MODEL, D_FF), layer),
                  _resident((D_FF, D_MODEL), layer), _resident((1, D_MODEL)), _resident((1, D_MODEL))],
        out_specs=row_spec,
        out_shape=jax.ShapeDtypeStruct((rows, D_MODEL), F32),
        compiler_params=_params("parallel"),
        name="ffn_ln",
    )(x, wg, wu, wd, gain, bias)


def _store_kv_heads(lat_bf16, k_rope_bf16, wuk_ref, wuvt_ref, kcat_ref, vt_ref):
    k_nope = _dot(lat_bf16, wuk_ref[...])
    v_t = _dot_nt(wuvt_ref[...], lat_bf16)
    ones = jnp.ones((V_ROWS - V_HEAD_DIM, lat_bf16.shape[0]), BF16)
    for h in range(N_HEADS):
        kcat_ref[h, :, 0:QK_NOPE_DIM] = k_nope[:, h * QK_NOPE_DIM:(h + 1) * QK_NOPE_DIM].astype(BF16)
        kcat_ref[h, :, QK_NOPE_DIM:QK_DIM] = k_rope_bf16
        vt_ref[h, 0:V_HEAD_DIM, :] = v_t[h * V_HEAD_DIM:(h + 1) * V_HEAD_DIM, :].astype(BF16)
        vt_ref[h, V_HEAD_DIM:V_ROWS, :] = ones


def _mla_in_kernel(x_ref, rope_ref, ropet_ref, wa_ref, qgain_ref, kvgain_ref, wuqt_ref, wuk_ref, wuvt_ref,
                   lat_stack_ref, krope_stack_ref, qt_ref, kcat_ref, vt_ref, lat_ref, krope_ref):
    del lat_stack_ref, krope_stack_ref
    xb = x_ref[...].astype(BF16)
    h = _dot(xb, wa_ref[...])
    q_lat = h[:, 0:Q_LORA_RANK]
    kv_lat = h[:, Q_LORA_RANK:Q_LORA_RANK + KV_LORA_RANK]
    k_pair = h[:, Q_LORA_RANK + KV_LORA_RANK:MLA_IN_COLS]

    w = k_pair * rope_ref[...]
    k_rope = (w + pltpu.roll(w, QK_ROPE_DIM, axis=1))[:, 0:QK_ROPE_DIM]
    krope_ref[...] = k_rope
    c_kv = _rms_norm(kv_lat, kvgain_ref[...])
    lat_ref[...] = c_kv
    _store_kv_heads(c_kv.astype(BF16), k_rope.astype(BF16), wuk_ref, wuvt_ref, kcat_ref, vt_ref)

    qn = _rms_norm(q_lat, qgain_ref[...]).astype(BF16)
    q_t = _dot_nt(wuqt_ref[...], qn) * EXP2_SCALE
    cos_t = ropet_ref[0:QK_ROPE_DIM, :]
    sin_t = ropet_ref[QK_ROPE_DIM:2 * QK_ROPE_DIM, :]
    for hd in range(N_HEADS):
        base = hd * Q_GROUP
        rope0 = base + QK_NOPE_DIM
        qt_ref[hd, 0:QK_NOPE_DIM, :] = q_t[base:rope0, :].astype(BF16)
        roped = q_t[rope0:rope0 + QK_ROPE_DIM, :] * cos_t + q_t[rope0 + QK_ROPE_DIM:base + Q_GROUP, :] * sin_t
        qt_ref[hd, QK_NOPE_DIM:QK_DIM, :] = roped.astype(BF16)


def _mla_in(x, rope_table, rope_table_t, wa, qgain, kvgain, wuqt, wuk, wuvt, lat_stack, krope_stack, *,
            layer, batch, seq, tm):
    tiles = seq // tm
    rows = batch * seq
    return pl.pallas_call(
        _mla_in_kernel,
        grid=(rows // tm,),
        in_specs=[pl.BlockSpec((tm, D_MODEL), lambda i: (i, 0)),
                  pl.BlockSpec((tm, 2 * QK_ROPE_DIM), lambda i: (i % tiles, 0)),
                  pl.BlockSpec((2 * QK_ROPE_DIM, tm), lambda i: (0, i % tiles)),
                  _resident((D_MODEL, MLA_IN_COLS), layer), _resident((1, Q_LORA_RANK)),
                  _resident((1, KV_LORA_RANK)), _resident((N_HEADS * Q_GROUP, Q_LORA_RANK), layer),
                  _resident((KV_LORA_RANK, N_HEADS * QK_NOPE_DIM), layer),
                  _resident((N_HEADS * V_HEAD_DIM, KV_LORA_RANK), layer), STACK_SPEC, STACK_SPEC],
        out_specs=[pl.BlockSpec((None, N_HEADS, QK_DIM, tm), lambda i: (i // tiles, 0, 0, i % tiles)),
                   pl.BlockSpec((None, N_HEADS, tm, QK_DIM), lambda i: (i // tiles, 0, i % tiles, 0)),
                   pl.BlockSpec((None, N_HEADS, None, V_ROWS, tm), lambda i: (i // tiles, 0, i % tiles, 0, 0)),
                   pl.BlockSpec((None, tm, KV_LORA_RANK), lambda i: (layer, i, 0)),
                   pl.BlockSpec((None, tm, QK_ROPE_DIM), lambda i: (layer, i, 0))],
        out_shape=[jax.ShapeDtypeStruct((batch, N_HEADS, QK_DIM, seq), BF16),
                   jax.ShapeDtypeStruct((batch, N_HEADS, seq, QK_DIM), BF16),
                   jax.ShapeDtypeStruct((batch, N_HEADS, tiles, V_ROWS, tm), BF16),
                   jax.ShapeDtypeStruct((DEPTH, rows, KV_LORA_RANK), F32),
                   jax.ShapeDtypeStruct((DEPTH, rows, QK_ROPE_DIM), F32)],
        input_output_aliases={9: 3, 10: 4},
        compiler_params=_params("parallel"),
        name="mla_in",
    )(x, rope_table, rope_table_t, wa, qgain, kvgain, wuqt, wuk, wuvt, lat_stack, krope_stack)


def _kv_past_kernel(lat_ref, krope_ref, wuk_ref, wuvt_ref, kcat_ref, vt_ref):
    _store_kv_heads(lat_ref[...].astype(BF16), krope_ref[...].astype(BF16), wuk_ref, wuvt_ref, kcat_ref, vt_ref)


def _kv_past(lat, krope, wuk, wuvt, *, layer, batch, seq):
    return pl.pallas_call(
        _kv_past_kernel,
        grid=(batch,),
        in_specs=[pl.BlockSpec((seq, KV_LORA_RANK), lambda b: (b, 0)),
                  pl.BlockSpec((seq, QK_ROPE_DIM), lambda b: (b, 0)),
                  _resident((KV_LORA_RANK, N_HEADS * QK_NOPE_DIM), layer),
                  _resident((N_HEADS * V_HEAD_DIM, KV_LORA_RANK), layer)],
        out_specs=[pl.BlockSpec((None, N_HEADS, seq, QK_DIM), lambda b: (b, 0, 0, 0)),
                   pl.BlockSpec((None, N_HEADS, V_ROWS, seq), lambda b: (b, 0, 0, 0))],
        out_shape=[jax.ShapeDtypeStruct((batch, N_HEADS, seq, QK_DIM), BF16),
                   jax.ShapeDtypeStruct((batch, N_HEADS, V_ROWS, seq), BF16)],
        compiler_params=_params("parallel"),
        name="kv_past",
    )(lat, krope, wuk, wuvt)


def _conv_in_kernel(x_ref, state_ref, wbcu_ref, convw_ref, wco_ref, state_stack_ref,
                    yconv_ref, newstate_ref, ubuf_ref, *, tm, tiles):
    del state_stack_ref
    xb = x_ref[...].astype(BF16)
    bcu = _dot(xb, wbcu_ref[...])
    conv_b = bcu[:, 0:D_CONV]
    u = bcu[:, D_CONV:2 * D_CONV] * bcu[:, 2 * D_CONV:3 * D_CONV]

    @pl.when(pl.program_id(0) % tiles == 0)
    def _():
        ubuf_ref[CONV_HALO - 2:CONV_HALO, :] = state_ref[...]

    ubuf_ref[CONV_HALO:CONV_HALO + tm, :] = u
    u_prev1 = ubuf_ref[CONV_HALO - 1:CONV_HALO - 1 + tm, :]
    u_prev2 = ubuf_ref[CONV_HALO - 2:CONV_HALO - 2 + tm, :]
    conv = convw_ref[2:3, :] * u + convw_ref[0:1, :] * u_prev2 + convw_ref[1:2, :] * u_prev1
    tail = u[tm - 2:tm, :]
    ubuf_ref[CONV_HALO - 2:CONV_HALO, :] = tail
    newstate_ref[...] = tail

    yconv_ref[...] = _dot((conv_b * conv).astype(BF16), wco_ref[...])


def _conv_in(x, state, wbcu, convw, wco, state_stack, *, layer, batch, seq, tm):
    tiles = seq // tm
    rows = batch * seq
    row_spec = pl.BlockSpec((tm, D_MODEL), lambda i: (i, 0))
    return pl.pallas_call(
        functools.partial(_conv_in_kernel, tm=tm, tiles=tiles),
        grid=(rows // tm,),
        in_specs=[row_spec, pl.BlockSpec((None, CONV_WIDTH - 1, D_CONV), lambda i: (i // tiles, 0, 0)),
                  _resident((D_MODEL, 3 * D_CONV), layer), _resident((CONV_WIDTH, D_CONV), layer),
                  _resident((D_CONV, D_MODEL), layer), STACK_SPEC],
        out_specs=[row_spec,
                   pl.BlockSpec((None, None, CONV_WIDTH - 1, D_CONV), lambda i: (layer, i // tiles, 0, 0))],
        out_shape=[jax.ShapeDtypeStruct((rows, D_MODEL), F32),
                   jax.ShapeDtypeStruct((DEPTH, batch, CONV_WIDTH - 1, D_CONV), F32)],
        scratch_shapes=[pltpu.VMEM((CONV_HALO + tm, D_CONV), F32)],
        input_output_aliases={5: 1},
        compiler_params=_params("arbitrary"),
        name="conv_in",
    )(x, state, wbcu, convw, wco, state_stack)


def _chunk_mask_t(k_pos0, q_pos0, tk, tq):
    k_chunk = (k_pos0 + lax.broadcasted_iota(jnp.int32, (tk, tq), 0)) // CHUNK
    q_chunk = (q_pos0 + lax.broadcasted_iota(jnp.int32, (tk, tq), 1)) // CHUNK
    return k_chunk <= q_chunk


def _prompt_attn_kernel(qt_ref, k_ref, vt_ref, o_ref, *scratch, tile, group, split):
    s_refs, acc_ref = scratch[:group], scratch[group]
    qi = pl.program_id(2)
    mask = _chunk_mask_t(0, 0, tile, tile)
    per = group // split
    subsets = [tuple(range(n * per, (n + 1) * per)) for n in range(split)]

    def key_tile(j, col_max, score_heads, value_heads, value_max, masked):
        start = pl.multiple_of(j * tile, tile)
        out = []
        for g, m_run in zip(score_heads, col_max):
            s_t = _dot(k_ref[g, pl.ds(start, tile), :], qt_ref[g])
            if masked:
                s_t = jnp.where(mask, s_t, MASK_VALUE)
            s_refs[g][j] = s_t
            out.append(jnp.maximum(m_run, jnp.max(s_t, axis=0, keepdims=True)))
        for g, m_fin in zip(value_heads, value_max):
            p_t = jnp.exp2(s_refs[g][j] - m_fin).astype(BF16)
            acc_ref[g] += _dot(vt_ref[g, j], p_t)
        return tuple(out)

    value_heads, value_max = (), ()
    for score_heads in subsets + [()]:
        for g in value_heads:
            acc_ref[g] = jnp.zeros((V_ROWS, tile), F32)
        step = functools.partial(key_tile, score_heads=score_heads, value_heads=value_heads,
                                 value_max=value_max)
        visible = functools.partial(step, masked=False)
        col_max = lax.fori_loop(0, qi // 2, lambda u, c: visible(2 * u + 1, visible(2 * u, c)),
                                tuple(jnp.full((1, tile), MASK_VALUE, F32) for _ in score_heads))
        col_max = lax.cond(qi % 2 == 1, lambda c: visible(qi - 1, c), lambda c: c, col_max)
        col_max = step(qi, col_max, masked=True)
        for g in value_heads:
            o_t = acc_ref[g, 0:V_HEAD_DIM, :] * (1.0 / acc_ref[g, V_HEAD_DIM:V_HEAD_DIM + 1, :])
            o_ref[:, g * V_HEAD_DIM:(g + 1) * V_HEAD_DIM] = o_t.T.astype(o_ref.dtype)
        value_heads, value_max = score_heads, col_max


def _prompt_attn(q_t, k, v_t, *, tile, group, split):
    batch, heads, _, seq = q_t.shape
    tiles = seq // tile
    return pl.pallas_call(
        functools.partial(_prompt_attn_kernel, tile=tile, group=group, split=split),
        grid=(batch, heads // group, tiles),
        in_specs=[pl.BlockSpec((None, group, QK_DIM, tile), lambda b, h, i: (b, h, 0, i)),
                  pl.BlockSpec((None, group, seq, QK_DIM), lambda b, h, i: (b, h, 0, 0),
                               pipeline_mode=pl.Buffered(1)),
                  pl.BlockSpec((None, group, tiles, V_ROWS, tile), lambda b, h, i: (b, h, 0, 0, 0),
                               pipeline_mode=pl.Buffered(1))],
        out_specs=pl.BlockSpec((tile, group * V_HEAD_DIM), lambda b, h, i: (b * tiles + i, h)),
        out_shape=jax.ShapeDtypeStruct((batch * seq, heads * V_HEAD_DIM), BF16),
        scratch_shapes=[pltpu.VMEM((tiles, tile, tile), F32) for _ in range(group)]
                       + [pltpu.VMEM((group, V_ROWS, tile), F32)],
        compiler_params=_params("parallel", "parallel", "arbitrary"),
        name="prompt_attn",
    )(q_t, k, v_t)


def _sample_attn_kernel(qt_ref, kp_ref, vpt_ref, kn_ref, vnt_ref, o_ref, *, past, new):
    mask_past = _chunk_mask_t(0, past, past, new)
    mask_new = _chunk_mask_t(past, past, new, new)
    for h in range(N_HEADS):
        q_t = qt_ref[h]
        s_past = jnp.where(mask_past, _dot(kp_ref[h], q_t), MASK_VALUE)
        s_new = jnp.where(mask_new, _dot(kn_ref[h], q_t), MASK_VALUE)
        m = jnp.maximum(jnp.max(s_past, axis=0, keepdims=True), jnp.max(s_new, axis=0, keepdims=True))
        p_past = jnp.exp2(s_past - m)
        p_new = jnp.exp2(s_new - m)
        denom = jnp.sum(p_past, axis=0, keepdims=True) + jnp.sum(p_new, axis=0, keepdims=True)
        o_t = (_dot(vpt_ref[h, 0:V_HEAD_DIM, :], p_past.astype(BF16))
               + _dot(vnt_ref[h, 0, 0:V_HEAD_DIM, :], p_new.astype(BF16)))
        o_ref[h] = (o_t * (1.0 / denom)).astype(o_ref.dtype)


def _sample_attn(q_t, k_past, v_past_t, k_new, v_new_t):
    batch, heads, _, new = q_t.shape
    past = k_past.shape[2]

    def spec(*dims):
        return pl.BlockSpec((None,) + dims, lambda b: (b,) + (0,) * len(dims))

    return pl.pallas_call(
        functools.partial(_sample_attn_kernel, past=past, new=new),
        grid=(batch,),
        in_specs=[spec(heads, QK_DIM, new), spec(heads, past, QK_DIM), spec(heads, V_ROWS, past),
                  spec(heads, new, QK_DIM), spec(heads, 1, V_ROWS, new)],
        out_specs=spec(heads, V_HEAD_DIM, new),
        out_shape=jax.ShapeDtypeStruct((batch, heads, V_HEAD_DIM, new), BF16),
        compiler_params=_params("parallel"),
        name="sample_attn",
    )(q_t, k_past, v_past_t, k_new, v_new_t)


def _mix_ln_kernel(o_ref, yconv_ref, x_ref, wg_ref, bgate_ref, wmo_ref, wmix_ref, gain_ref, bias_ref, out_ref,
                   *, parts):
    rows = x_ref.shape[0] // parts
    for part in range(parts):
        sl = slice(part * rows, (part + 1) * rows)
        x = x_ref[sl, :]
        gates = _dot(x.astype(BF16), wg_ref[...])
        gate_conv = jax.nn.sigmoid(gates[:, 0:D_MODEL] + bgate_ref[0:1, :])
        gate_mla = jax.nn.sigmoid(gates[:, D_MODEL:2 * D_MODEL] + bgate_ref[1:2, :])
        y_mla = _dot(o_ref[sl, :], wmo_ref[...])
        merged = (gate_conv * yconv_ref[sl, :] + gate_mla * y_mla).astype(BF16)
        mixed = _dot(merged, wmix_ref[...])
        out_ref[sl, :] = _layer_norm(DEEPNORM_ALPHA * x + mixed, gain_ref[...], bias_ref[...])


def _mix_ln(o, y_conv, x, wg, bgate, wmo, wmix, gain, bias, *, layer, tm, parts=2):
    rows = x.shape[0]
    row_spec = pl.BlockSpec((tm, D_MODEL), lambda i: (i, 0))
    return pl.pallas_call(
        functools.partial(_mix_ln_kernel, parts=parts),
        grid=(rows // tm,),
        in_specs=[row_spec, row_spec, row_spec, _resident((D_MODEL, 2 * D_MODEL), layer),
                  _resident((2, D_MODEL), layer), _resident((N_HEADS * V_HEAD_DIM, D_MODEL), layer),
                  _resident((D_MODEL, D_MODEL), layer), _resident((1, D_MODEL)), _resident((1, D_MODEL))],
        out_specs=row_spec,
        out_shape=jax.ShapeDtypeStruct((rows, D_MODEL), F32),
        compiler_params=_params("parallel"),
        name="mix_ln",
    )(o, y_conv, x, wg, bgate, wmo, wmix, gain, bias)


def _rope_tables(pos0, length):
    half = QK_ROPE_DIM // 2
    inv = ROPE_BASE ** (-jnp.arange(half, dtype=F32) / half)
    ang = (pos0 + jnp.arange(length)).astype(F32)[:, None] * inv[None, :]
    cos, sin = jnp.cos(ang), jnp.sin(ang)
    table = jnp.concatenate([cos, cos, sin, sin], axis=-1)
    return table, table.T


def _rotate_half_cols(w):
    half = QK_ROPE_DIM // 2
    return jnp.concatenate([-w[..., half:], w[..., :half]], axis=-1)


def _prepare_weights(ffn1_w_gate_up, ffn1_w_down, ffn2_w_gate_up, ffn2_w_down, w_in, w_uq, w_ukv,
                     w_mla_out, w_conv_out, w_mix_out):
    mla_cols = Q_LORA_RANK + KV_LORA_RANK + QK_ROPE_DIM
    k_rope_cols = w_in[:, :, Q_LORA_RANK + KV_LORA_RANK:mla_cols]
    wa = jnp.concatenate([w_in[:, :, :mla_cols], _rotate_half_cols(k_rope_cols)], axis=-1)
    uq = w_uq.reshape(DEPTH, Q_LORA_RANK, N_HEADS, QK_DIM)
    uq = jnp.concatenate([uq, _rotate_half_cols(uq[..., QK_NOPE_DIM:])], axis=-1)
    uq_t = uq.reshape(DEPTH, Q_LORA_RANK, N_HEADS * Q_GROUP).transpose(0, 2, 1)
    ukv = w_ukv.reshape(DEPTH, KV_LORA_RANK, N_HEADS, QK_NOPE_DIM + V_HEAD_DIM)
    uk = ukv[..., :QK_NOPE_DIM].reshape(DEPTH, KV_LORA_RANK, N_HEADS * QK_NOPE_DIM)
    uv_t = ukv[..., QK_NOPE_DIM:].reshape(DEPTH, KV_LORA_RANK, N_HEADS * V_HEAD_DIM).transpose(0, 2, 1)
    return dict(
        ffn1_g=ffn1_w_gate_up[:, :, :D_FF].astype(BF16), ffn1_u=ffn1_w_gate_up[:, :, D_FF:].astype(BF16),
        ffn1_d=ffn1_w_down.astype(BF16),
        ffn2_g=ffn2_w_gate_up[:, :, :D_FF].astype(BF16), ffn2_u=ffn2_w_gate_up[:, :, D_FF:].astype(BF16),
        ffn2_d=ffn2_w_down.astype(BF16),
        wa=wa.astype(BF16),
        wbcu=w_in[:, :, mla_cols:mla_cols + 3 * D_CONV].astype(BF16),
        wgates=w_in[:, :, mla_cols + 3 * D_CONV:].astype(BF16),
        wuqt=uq_t.astype(BF16), wuk=uk.astype(BF16), wuvt=uv_t.astype(BF16),
        wmo=w_mla_out.astype(BF16), wco=w_conv_out.astype(BF16), wmix=w_mix_out.astype(BF16),
    )


def _run_trunk(x, cache_lat, cache_kr, conv_state, w, ln_gain, ln_bias, b_gate, q_norm_gain, kv_norm_gain,
               conv_w, *, tm, ffn_tm, ffn_parts, seq_tm, attn_group, attn_split):
    batch, seq, _ = x.shape
    past = 0 if cache_lat is None else cache_lat.shape[2]
    rope_table, rope_table_t = _rope_tables(past, seq)
    x = x.reshape(batch * seq, D_MODEL)
    lat = jnp.zeros((DEPTH, batch * seq, KV_LORA_RANK), F32)
    kr = jnp.zeros((DEPTH, batch * seq, QK_ROPE_DIM), F32)
    conv_new = jnp.zeros((DEPTH, batch, CONV_WIDTH - 1, D_CONV), F32)
    for l in range(DEPTH):
        gain = lambda k: ln_gain[l, k][None, :]
        bias = lambda k: ln_bias[l, k][None, :]
        state = jnp.zeros((batch, CONV_WIDTH - 1, D_CONV), F32) if conv_state is None else conv_state[l]
        x = _ffn_ln(x, w["ffn1_g"], w["ffn1_u"], w["ffn1_d"], gain(0), bias(0), layer=l, tm=ffn_tm,
                    parts=ffn_parts)
        q_t, k_new, v_new_t, lat, kr = _mla_in(
            x, rope_table, rope_table_t, w["wa"], q_norm_gain[l][None, :], kv_norm_gain[l][None, :],
            w["wuqt"], w["wuk"], w["wuvt"], lat, kr, layer=l, batch=batch, seq=seq, tm=seq_tm)
        y_conv, conv_new = _conv_in(x, state, w["wbcu"], conv_w, w["wco"], conv_new,
                                    layer=l, batch=batch, seq=seq, tm=seq_tm)
        if cache_lat is None:
            o = _prompt_attn(q_t, k_new, v_new_t, tile=seq_tm, group=attn_group, split=attn_split[l])
        else:
            k_past, v_past_t = _kv_past(cache_lat[l].reshape(batch * past, KV_LORA_RANK),
                                        cache_kr[l].reshape(batch * past, QK_ROPE_DIM),
                                        w["wuk"], w["wuvt"], layer=l, batch=batch, seq=past)
            o_t = _sample_attn(q_t, k_past, v_past_t, k_new, v_new_t)
            o = o_t.transpose(0, 3, 1, 2).reshape(batch * seq, N_HEADS * V_HEAD_DIM)
        x = _mix_ln(o, y_conv, x, w["wgates"], b_gate, w["wmo"], w["wmix"], gain(1), bias(1), layer=l, tm=tm)
        x = _ffn_ln(x, w["ffn2_g"], w["ffn2_u"], w["ffn2_d"], gain(2), bias(2), layer=l, tm=ffn_tm,
                    parts=ffn_parts)
    return (x.reshape(batch, seq, D_MODEL), lat.reshape(DEPTH, batch, seq, KV_LORA_RANK),
            kr.reshape(DEPTH, batch, seq, QK_ROPE_DIM), conv_new)


def kernel(x_prompt, x_sample, cache_kv_latent, cache_k_rope, state_conv, ffn1_w_gate_up, ffn1_w_down,
           ffn2_w_gate_up, ffn2_w_down, ln_gain, ln_bias, w_in, b_gate, q_norm_gain, kv_norm_gain, w_uq, w_ukv,
           w_mla_out, conv_w, w_conv_out, w_mix_out):
    w = _prepare_weights(ffn1_w_gate_up, ffn1_w_down, ffn2_w_gate_up, ffn2_w_down, w_in, w_uq, w_ukv,
                         w_mla_out, w_conv_out, w_mix_out)
    shared = (w, ln_gain, ln_bias, b_gate, q_norm_gain, kv_norm_gain, conv_w)
    y_prompt, p_lat, p_kr, p_conv = _run_trunk(x_prompt, None, None, None, *shared,
                                               tm=512, ffn_tm=1024, ffn_parts=4, seq_tm=512, attn_group=4, attn_split=(1, 1, 1, 1))
    y_sample, s_lat, s_kr, s_conv = _run_trunk(x_sample, cache_kv_latent, cache_k_rope, state_conv, *shared,
                                               tm=512, ffn_tm=512, ffn_parts=2, seq_tm=x_sample.shape[1], attn_group=None, attn_split=None)
    return (y_prompt, y_sample, p_lat, p_kr, p_conv, s_lat, s_kr, s_conv)
```

```python
import functools
import math

import jax
import jax.numpy as jnp
from jax import lax
from jax.experimental import pallas as pl
from jax.experimental.pallas import tpu as pltpu

D_MODEL = 1024
DEPTH = 4
CHUNK = 64
N_HEADS = 8
QK_NOPE_DIM = 128
QK_ROPE_DIM = 64
V_HEAD_DIM = 128
Q_LORA_RANK = 384
KV_LORA_RANK = 256
ROPE_BASE = 10000.0
D_CONV = D_MODEL
CONV_WIDTH = 3
D_FF = 2816
NORM_EPS = 1e-5
QK_DIM = QK_NOPE_DIM + QK_ROPE_DIM
ATTN_SCALE = QK_DIM ** -0.5
EXP2_SCALE = ATTN_SCALE * math.log2(math.e)
DEEPNORM_ALPHA = (2 * DEPTH) ** 0.25
MASK_VALUE = -1e30

Q_GROUP = QK_NOPE_DIM + 2 * QK_ROPE_DIM
MLA_IN_COLS = Q_LORA_RANK + KV_LORA_RANK + 2 * QK_ROPE_DIM
CONV_HALO = 8
BF16_SUBLANES = 16
V_ROWS = V_HEAD_DIM + BF16_SUBLANES

VMEM_LIMIT_BYTES = 56 * 1024 * 1024

F32 = jnp.float32
BF16 = jnp.bfloat16


def _dot(a, b):
    return jnp.dot(a, b, preferred_element_type=F32)


def _dot_nt(a, b):
    return lax.dot_general(a, b, (((1,), (1,)), ((), ())), preferred_element_type=F32)


def _layer_norm(z, gain, bias):
    mu = jnp.mean(z, axis=-1, keepdims=True)
    zc = z - mu
    var = jnp.mean(zc * zc, axis=-1, keepdims=True)
    return zc * lax.rsqrt(var + NORM_EPS) * gain + bias


def _rms_norm(z, gain):
    ms = jnp.mean(z * z, axis=-1, keepdims=True)
    return z * lax.rsqrt(ms + NORM_EPS) * gain


def _resident(shape, *lead, col=0):
    index = tuple(lead) + (0,) * (len(shape) - 1) + (col,)
    return pl.BlockSpec((None,) * len(lead) + tuple(shape), lambda *_: index, pipeline_mode=pl.Buffered(1))


STACK_SPEC = pl.BlockSpec(memory_space=pl.ANY)


def _params(*semantics):
    return pltpu.CompilerParams(dimension_semantics=semantics, vmem_limit_bytes=VMEM_LIMIT_BYTES)


def _ffn_ln_kernel(x_ref, wg_ref, wu_ref, wd_ref, gain_ref, bias_ref, o_ref, *, parts):
    rows = x_ref.shape[0] // parts
    for part in range(parts):
        sl = slice(part * rows, (part + 1) * rows)
        x = x_ref[sl, :]
        xb = x.astype(BF16)
        gate = _dot(xb, wg_ref[...])
        up = _dot(xb, wu_ref[...])
        hidden = (gate * jax.nn.sigmoid(gate) * up).astype(BF16)
        y = _dot(hidden, wd_ref[...])
        o_ref[sl, :] = _layer_norm(DEEPNORM_ALPHA * x + 0.5 * y, gain_ref[...], bias_ref[...])


def _ffn_ln(x, wgu, wd, gain, bias, *, layer, tm, parts):
    rows = x.shape[0]
    row_spec = pl.BlockSpec((tm, D_MODEL), lambda i: (i, 0))
    return pl.pallas_call(
        functools.partial(_ffn_ln_kernel, parts=parts),
        grid=(rows // tm,),
        in_specs=[row_spec, _resident((D_MODEL, D_FF), layer, col=0), _resident((D_MODEL, D_FF), layer, col=1),
                  _resident((D_FF, D_MODEL), layer), _resident((1, D_MODEL)), _resident((1, D_MODEL))],
        out_specs=row_spec,
        out_shape=jax.ShapeDtypeStruct((rows, D_MODEL), F32),
        compiler_params=_params("parallel"),
        name="ffn_ln",
    )(x, wgu, wgu, wd, gain, bias)


def _store_kv_heads(lat_bf16, k_rope_bf16, wuk_ref, wuvt_ref, kcat_ref, vt_ref):
    k_nope = _dot(lat_bf16, wuk_ref[...])
    v_t = _dot_nt(wuvt_ref[...], lat_bf16)
    ones = jnp.ones((V_ROWS - V_HEAD_DIM, lat_bf16.shape[0]), BF16)
    for h in range(N_HEADS):
        kcat_ref[h, :, 0:QK_NOPE_DIM] = k_nope[:, h * QK_NOPE_DIM:(h + 1) * QK_NOPE_DIM].astype(BF16)
        kcat_ref[h, :, QK_NOPE_DIM:QK_DIM] = k_rope_bf16
        vt_ref[h, 0:V_HEAD_DIM, :] = v_t[h * V_HEAD_DIM:(h + 1) * V_HEAD_DIM, :].astype(BF16)
        vt_ref[h, V_HEAD_DIM:V_ROWS, :] = ones


def _mla_in_kernel(x_ref, rope_ref, ropet_ref, wa_ref, qgain_ref, kvgain_ref, wuqt_ref, wuk_ref, wuvt_ref,
                   lat_stack_ref, krope_stack_ref, qt_ref, kcat_ref, vt_ref, lat_ref, krope_ref):
    del lat_stack_ref, krope_stack_ref
    xb = x_ref[...].astype(BF16)
    h = _dot(xb, wa_ref[...])
    q_lat = h[:, 0:Q_LORA_RANK]
    kv_lat = h[:, Q_LORA_RANK:Q_LORA_RANK + KV_LORA_RANK]
    k_pair = h[:, Q_LORA_RANK + KV_LORA_RANK:MLA_IN_COLS]

    w = k_pair * rope_ref[...]
    k_rope = (w + pltpu.roll(w, QK_ROPE_DIM, axis=1))[:, 0:QK_ROPE_DIM]
    krope_ref[...] = k_rope
    c_kv = _rms_norm(kv_lat, kvgain_ref[...])
    lat_ref[...] = c_kv
    _store_kv_heads(c_kv.astype(BF16), k_rope.astype(BF16), wuk_ref, wuvt_ref, kcat_ref, vt_ref)

    qn = _rms_norm(q_lat, qgain_ref[...]).astype(BF16)
    q_t = _dot_nt(wuqt_ref[...], qn) * EXP2_SCALE
    cos_t = ropet_ref[0:QK_ROPE_DIM, :]
    sin_t = ropet_ref[QK_ROPE_DIM:2 * QK_ROPE_DIM, :]
    for hd in range(N_HEADS):
        base = hd * Q_GROUP
        rope0 = base + QK_NOPE_DIM
        qt_ref[hd, 0:QK_NOPE_DIM, :] = q_t[base:rope0, :].astype(BF16)
        roped = q_t[rope0:rope0 + QK_ROPE_DIM, :] * cos_t + q_t[rope0 + QK_ROPE_DIM:base + Q_GROUP, :] * sin_t
        qt_ref[hd, QK_NOPE_DIM:QK_DIM, :] = roped.astype(BF16)


def _mla_in(x, rope_table, rope_table_t, wa, qgain, kvgain, wuqt, wuk, wuvt, lat_stack, krope_stack, *,
            layer, batch, seq, tm):
    tiles = seq // tm
    rows = batch * seq
    return pl.pallas_call(
        _mla_in_kernel,
        grid=(rows // tm,),
        in_specs=[pl.BlockSpec((tm, D_MODEL), lambda i: (i, 0)),
                  pl.BlockSpec((tm, 2 * QK_ROPE_DIM), lambda i: (i % tiles, 0)),
                  pl.BlockSpec((2 * QK_ROPE_DIM, tm), lambda i: (0, i % tiles)),
                  _resident((D_MODEL, MLA_IN_COLS), layer), _resident((1, Q_LORA_RANK)),
                  _resident((1, KV_LORA_RANK)), _resident((N_HEADS * Q_GROUP, Q_LORA_RANK), layer),
                  _resident((KV_LORA_RANK, N_HEADS * QK_NOPE_DIM), layer),
                  _resident((N_HEADS * V_HEAD_DIM, KV_LORA_RANK), layer), STACK_SPEC, STACK_SPEC],
        out_specs=[pl.BlockSpec((None, N_HEADS, QK_DIM, tm), lambda i: (i // tiles, 0, 0, i % tiles)),
                   pl.BlockSpec((None, N_HEADS, tm, QK_DIM), lambda i: (i // tiles, 0, i % tiles, 0)),
                   pl.BlockSpec((None, N_HEADS, None, V_ROWS, tm), lambda i: (i // tiles, 0, i % tiles, 0, 0)),
                   pl.BlockSpec((None, tm, KV_LORA_RANK), lambda i: (layer, i, 0)),
                   pl.BlockSpec((None, tm, QK_ROPE_DIM), lambda i: (layer, i, 0))],
        out_shape=[jax.ShapeDtypeStruct((batch, N_HEADS, QK_DIM, seq), BF16),
                   jax.ShapeDtypeStruct((batch, N_HEADS, seq, QK_DIM), BF16),
                   jax.ShapeDtypeStruct((batch, N_HEADS, tiles, V_ROWS, tm), BF16),
                   jax.ShapeDtypeStruct((DEPTH, rows, KV_LORA_RANK), F32),
                   jax.ShapeDtypeStruct((DEPTH, rows, QK_ROPE_DIM), F32)],
        input_output_aliases={9: 3, 10: 4},
        compiler_params=_params("parallel"),
        name="mla_in",
    )(x, rope_table, rope_table_t, wa, qgain, kvgain, wuqt, wuk, wuvt, lat_stack, krope_stack)


def _kv_past_kernel(lat_ref, krope_ref, wuk_ref, wuvt_ref, kcat_ref, vt_ref):
    _store_kv_heads(lat_ref[...].astype(BF16), krope_ref[...].astype(BF16), wuk_ref, wuvt_ref, kcat_ref, vt_ref)


def _kv_past(lat, krope, wuk, wuvt, *, layer, batch, seq):
    return pl.pallas_call(
        _kv_past_kernel,
        grid=(batch,),
        in_specs=[pl.BlockSpec((seq, KV_LORA_RANK), lambda b: (b, 0)),
                  pl.BlockSpec((seq, QK_ROPE_DIM), lambda b: (b, 0)),
                  _resident((KV_LORA_RANK, N_HEADS * QK_NOPE_DIM), layer),
                  _resident((N_HEADS * V_HEAD_DIM, KV_LORA_RANK), layer)],
        out_specs=[pl.BlockSpec((None, N_HEADS, seq, QK_DIM), lambda b: (b, 0, 0, 0)),
                   pl.BlockSpec((None, N_HEADS, V_ROWS, seq), lambda b: (b, 0, 0, 0))],
        out_shape=[jax.ShapeDtypeStruct((batch, N_HEADS, seq, QK_DIM), BF16),
                   jax.ShapeDtypeStruct((batch, N_HEADS, V_ROWS, seq), BF16)],
        compiler_params=_params("parallel"),
        name="kv_past",
    )(lat, krope, wuk, wuvt)


def _conv_in_kernel(x_ref, state_ref, wbcu_ref, convw_ref, wco_ref, state_stack_ref,
                    yconv_ref, newstate_ref, ubuf_ref, *, tm, tiles):
    del state_stack_ref
    xb = x_ref[...].astype(BF16)
    bcu = _dot(xb, wbcu_ref[...])
    conv_b = bcu[:, 0:D_CONV]
    u = bcu[:, D_CONV:2 * D_CONV] * bcu[:, 2 * D_CONV:3 * D_CONV]

    @pl.when(pl.program_id(0) % tiles == 0)
    def _():
        ubuf_ref[CONV_HALO - 2:CONV_HALO, :] = state_ref[...]

    ubuf_ref[CONV_HALO:CONV_HALO + tm, :] = u
    u_prev1 = ubuf_ref[CONV_HALO - 1:CONV_HALO - 1 + tm, :]
    u_prev2 = ubuf_ref[CONV_HALO - 2:CONV_HALO - 2 + tm, :]
    conv = convw_ref[2:3, :] * u + convw_ref[0:1, :] * u_prev2 + convw_ref[1:2, :] * u_prev1
    tail = u[tm - 2:tm, :]
    ubuf_ref[CONV_HALO - 2:CONV_HALO, :] = tail
    newstate_ref[...] = tail

    yconv_ref[...] = _dot((conv_b * conv).astype(BF16), wco_ref[...])


def _conv_in(x, state, wbcu, convw, wco, state_stack, *, layer, batch, seq, tm):
    tiles = seq // tm
    rows = batch * seq
    row_spec = pl.BlockSpec((tm, D_MODEL), lambda i: (i, 0))
    return pl.pallas_call(
        functools.partial(_conv_in_kernel, tm=tm, tiles=tiles),
        grid=(rows // tm,),
        in_specs=[row_spec, pl.BlockSpec((None, CONV_WIDTH - 1, D_CONV), lambda i: (i // tiles, 0, 0)),
                  _resident((D_MODEL, 3 * D_CONV), layer), _resident((CONV_WIDTH, D_CONV), layer),
                  _resident((D_CONV, D_MODEL), layer), STACK_SPEC],
        out_specs=[row_spec,
                   pl.BlockSpec((None, None, CONV_WIDTH - 1, D_CONV), lambda i: (layer, i // tiles, 0, 0))],
        out_shape=[jax.ShapeDtypeStruct((rows, D_MODEL), F32),
                   jax.ShapeDtypeStruct((DEPTH, batch, CONV_WIDTH - 1, D_CONV), F32)],
        scratch_shapes=[pltpu.VMEM((CONV_HALO + tm, D_CONV), F32)],
        input_output_aliases={5: 1},
        compiler_params=_params("arbitrary"),
        name="conv_in",
    )(x, state, wbcu, convw, wco, state_stack)


def _chunk_mask_t(k_pos0, q_pos0, tk, tq):
    k_chunk = (k_pos0 + lax.broadcasted_iota(jnp.int32, (tk, tq), 0)) // CHUNK
    q_chunk = (q_pos0 + lax.broadcasted_iota(jnp.int32, (tk, tq), 1)) // CHUNK
    return k_chunk <= q_chunk


def _prompt_attn_kernel(qt_ref, k_ref, vt_ref, o_ref, *scratch, tile, group):
    s_refs, acc_ref = scratch[:group], scratch[group]
    qi = pl.program_id(2)
    mask = _chunk_mask_t(0, 0, tile, tile)

    def over_key_tiles(tile_fn, carry):
        visible = functools.partial(tile_fn, masked=False)
        carry = lax.fori_loop(0, qi // 2, lambda u, c: visible(2 * u + 1, visible(2 * u, c)), carry)
        carry = lax.cond(qi % 2 == 1, lambda c: visible(qi - 1, c), lambda c: c, carry)
        return tile_fn(qi, carry, masked=True)

    def score_tile(j, col_max, masked):
        start = pl.multiple_of(j * tile, tile)
        out = []
        for g in range(group):
            s_t = _dot(k_ref[g, pl.ds(start, tile), :], qt_ref[g])
            if masked:
                s_t = jnp.where(mask, s_t, MASK_VALUE)
            s_refs[g][j] = s_t
            out.append(jnp.maximum(col_max[g], jnp.max(s_t, axis=0, keepdims=True)))
        return tuple(out)

    col_max = over_key_tiles(score_tile, tuple(jnp.full((1, tile), MASK_VALUE, F32) for _ in range(group)))

    def value_tile(j, carry, masked):
        del masked
        for g in range(group):
            p_t = jnp.exp2(s_refs[g][j] - col_max[g]).astype(BF16)
            acc_ref[g] += _dot(vt_ref[g, j], p_t)
        return carry

    acc_ref[...] = jnp.zeros_like(acc_ref)
    over_key_tiles(value_tile, ())
    for g in range(group):
        o_t = acc_ref[g, 0:V_HEAD_DIM, :] * (1.0 / acc_ref[g, V_HEAD_DIM:V_HEAD_DIM + 1, :])
        o_ref[:, g * V_HEAD_DIM:(g + 1) * V_HEAD_DIM] = o_t.T.astype(o_ref.dtype)


def _prompt_attn(q_t, k, v_t, *, tile, group):
    batch, heads, _, seq = q_t.shape
    tiles = seq // tile
    return pl.pallas_call(
        functools.partial(_prompt_attn_kernel, tile=tile, group=group),
        grid=(batch, heads // group, tiles),
        in_specs=[pl.BlockSpec((None, group, QK_DIM, tile), lambda b, h, i: (b, h, 0, i)),
                  pl.BlockSpec((None, group, seq, QK_DIM), lambda b, h, i: (b, h, 0, 0),
                               pipeline_mode=pl.Buffered(1)),
                  pl.BlockSpec((None, group, tiles, V_ROWS, tile), lambda b, h, i: (b, h, 0, 0, 0),
                               pipeline_mode=pl.Buffered(1))],
        out_specs=pl.BlockSpec((tile, group * V_HEAD_DIM), lambda b, h, i: (b * tiles + i, h)),
        out_shape=jax.ShapeDtypeStruct((batch * seq, heads * V_HEAD_DIM), BF16),
        scratch_shapes=[pltpu.VMEM((tiles, tile, tile), F32) for _ in range(group)]
                       + [pltpu.VMEM((group, V_ROWS, tile), F32)],
        compiler_params=_params("parallel", "parallel", "arbitrary"),
        name="prompt_attn",
    )(q_t, k, v_t)


def _sample_attn_kernel(qt_ref, kp_ref, vpt_ref, kn_ref, vnt_ref, o_ref, *, past, new):
    mask_past = _chunk_mask_t(0, past, past, new)
    mask_new = _chunk_mask_t(past, past, new, new)
    for h in range(N_HEADS):
        q_t = qt_ref[h]
        s_past = jnp.where(mask_past, _dot(kp_ref[h], q_t), MASK_VALUE)
        s_new = jnp.where(mask_new, _dot(kn_ref[h], q_t), MASK_VALUE)
        m = jnp.maximum(jnp.max(s_past, axis=0, keepdims=True), jnp.max(s_new, axis=0, keepdims=True))
        p_past = jnp.exp2(s_past - m)
        p_new = jnp.exp2(s_new - m)
        denom = jnp.sum(p_past, axis=0, keepdims=True) + jnp.sum(p_new, axis=0, keepdims=True)
        o_t = (_dot(vpt_ref[h, 0:V_HEAD_DIM, :], p_past.astype(BF16))
               + _dot(vnt_ref[h, 0, 0:V_HEAD_DIM, :], p_new.astype(BF16)))
        o_ref[h] = (o_t * (1.0 / denom)).astype(o_ref.dtype)


def _sample_attn(q_t, k_past, v_past_t, k_new, v_new_t):
    batch, heads, _, new = q_t.shape
    past = k_past.shape[2]

    def spec(*dims):
        return pl.BlockSpec((None,) + dims, lambda b: (b,) + (0,) * len(dims))

    return pl.pallas_call(
        functools.partial(_sample_attn_kernel, past=past, new=new),
        grid=(batch,),
        in_specs=[spec(heads, QK_DIM, new), spec(heads, past, QK_DIM), spec(heads, V_ROWS, past),
                  spec(heads, new, QK_DIM), spec(heads, 1, V_ROWS, new)],
        out_specs=spec(heads, V_HEAD_DIM, new),
        out_shape=jax.ShapeDtypeStruct((batch, heads, V_HEAD_DIM, new), BF16),
        compiler_params=_params("parallel"),
        name="sample_attn",
    )(q_t, k_past, v_past_t, k_new, v_new_t)


def _mix_ln_kernel(o_ref, yconv_ref, x_ref, wg_ref, bgate_ref, wmo_ref, wmix_ref, gain_ref, bias_ref, out_ref,
                   *, parts):
    rows = x_ref.shape[0] // parts
    for part in range(parts):
        sl = slice(part * rows, (part + 1) * rows)
        x = x_ref[sl, :]
        gates = _dot(x.astype(BF16), wg_ref[...])
        gate_conv = jax.nn.sigmoid(gates[:, 0:D_MODEL] + bgate_ref[0:1, :])
        gate_mla = jax.nn.sigmoid(gates[:, D_MODEL:2 * D_MODEL] + bgate_ref[1:2, :])
        y_mla = _dot(o_ref[sl, :], wmo_ref[...])
        merged = (gate_conv * yconv_ref[sl, :] + gate_mla * y_mla).astype(BF16)
        mixed = _dot(merged, wmix_ref[...])
        out_ref[sl, :] = _layer_norm(DEEPNORM_ALPHA * x + mixed, gain_ref[...], bias_ref[...])


def _mix_ln(o, y_conv, x, wg, bgate, wmo, wmix, gain, bias, *, layer, tm, parts=2):
    rows = x.shape[0]
    row_spec = pl.BlockSpec((tm, D_MODEL), lambda i: (i, 0))
    return pl.pallas_call(
        functools.partial(_mix_ln_kernel, parts=parts),
        grid=(rows // tm,),
        in_specs=[row_spec, row_spec, row_spec, _resident((D_MODEL, 2 * D_MODEL), layer),
                  _resident((2, D_MODEL), layer), _resident((N_HEADS * V_HEAD_DIM, D_MODEL), layer),
                  _resident((D_MODEL, D_MODEL), layer), _resident((1, D_MODEL)), _resident((1, D_MODEL))],
        out_specs=row_spec,
        out_shape=jax.ShapeDtypeStruct((rows, D_MODEL), F32),
        compiler_params=_params("parallel"),
        name="mix_ln",
    )(o, y_conv, x, wg, bgate, wmo, wmix, gain, bias)


def _rope_tables(pos0, length):
    half = QK_ROPE_DIM // 2
    inv = ROPE_BASE ** (-jnp.arange(half, dtype=F32) / half)
    ang = (pos0 + jnp.arange(length)).astype(F32)[:, None] * inv[None, :]
    cos, sin = jnp.cos(ang), jnp.sin(ang)
    table = jnp.concatenate([cos, cos, sin, sin], axis=-1)
    return table, table.T


def _rotate_half_cols(w):
    half = QK_ROPE_DIM // 2
    return jnp.concatenate([-w[..., half:], w[..., :half]], axis=-1)


def _prepare_weights(ffn1_w_gate_up, ffn1_w_down, ffn2_w_gate_up, ffn2_w_down, w_in, w_uq, w_ukv,
                     w_mla_out, w_conv_out, w_mix_out):
    mla_cols = Q_LORA_RANK + KV_LORA_RANK + QK_ROPE_DIM
    k_rope_cols = w_in[:, :, Q_LORA_RANK + KV_LORA_RANK:mla_cols]
    wa = jnp.concatenate([w_in[:, :, :mla_cols], _rotate_half_cols(k_rope_cols)], axis=-1)
    uq = w_uq.reshape(DEPTH, Q_LORA_RANK, N_HEADS, QK_DIM)
    uq = jnp.concatenate([uq, _rotate_half_cols(uq[..., QK_NOPE_DIM:])], axis=-1)
    uq_t = uq.reshape(DEPTH, Q_LORA_RANK, N_HEADS * Q_GROUP).transpose(0, 2, 1)
    ukv = w_ukv.reshape(DEPTH, KV_LORA_RANK, N_HEADS, QK_NOPE_DIM + V_HEAD_DIM)
    uk = ukv[..., :QK_NOPE_DIM].reshape(DEPTH, KV_LORA_RANK, N_HEADS * QK_NOPE_DIM)
    uv_t = ukv[..., QK_NOPE_DIM:].reshape(DEPTH, KV_LORA_RANK, N_HEADS * V_HEAD_DIM).transpose(0, 2, 1)
    return dict(
        ffn1_gu=ffn1_w_gate_up.astype(BF16), ffn1_d=ffn1_w_down.astype(BF16),
        ffn2_gu=ffn2_w_gate_up.astype(BF16), ffn2_d=ffn2_w_down.astype(BF16),
        wa=wa.astype(BF16),
        wbcu=w_in[:, :, mla_cols:mla_cols + 3 * D_CONV].astype(BF16),
        wgates=w_in[:, :, mla_cols + 3 * D_CONV:].astype(BF16),
        wuqt=uq_t.astype(BF16), wuk=uk.astype(BF16), wuvt=uv_t.astype(BF16),
        wmo=w_mla_out.astype(BF16), wco=w_conv_out.astype(BF16), wmix=w_mix_out.astype(BF16),
    )


def _run_trunk(x, cache_lat, cache_kr, conv_state, w, ln_gain, ln_bias, b_gate, q_norm_gain, kv_norm_gain,
               conv_w, *, tm, ffn_tm, ffn_parts, seq_tm, attn_group):
    batch, seq, _ = x.shape
    past = 0 if cache_lat is None else cache_lat.shape[2]
    rope_table, rope_table_t = _rope_tables(past, seq)
    x = x.reshape(batch * seq, D_MODEL)
    lat = jnp.zeros((DEPTH, batch * seq, KV_LORA_RANK), F32)
    kr = jnp.zeros((DEPTH, batch * seq, QK_ROPE_DIM), F32)
    conv_new = jnp.zeros((DEPTH, batch, CONV_WIDTH - 1, D_CONV), F32)
    for l in range(DEPTH):
        gain = lambda k: ln_gain[l, k][None, :]
        bias = lambda k: ln_bias[l, k][None, :]
        state = jnp.zeros((batch, CONV_WIDTH - 1, D_CONV), F32) if conv_state is None else conv_state[l]
        x = _ffn_ln(x, w["ffn1_gu"], w["ffn1_d"], gain(0), bias(0), layer=l, tm=ffn_tm,
                    parts=ffn_parts)
        q_t, k_new, v_new_t, lat, kr = _mla_in(
            x, rope_table, rope_table_t, w["wa"], q_norm_gain[l][None, :], kv_norm_gain[l][None, :],
            w["wuqt"], w["wuk"], w["wuvt"], lat, kr, layer=l, batch=batch, seq=seq, tm=seq_tm)
        y_conv, conv_new = _conv_in(x, state, w["wbcu"], conv_w, w["wco"], conv_new,
                                    layer=l, batch=batch, seq=seq, tm=seq_tm)
        if cache_lat is None:
            o = _prompt_attn(q_t, k_new, v_new_t, tile=seq_tm, group=attn_group)
        else:
            k_past, v_past_t = _kv_past(cache_lat[l].reshape(batch * past, KV_LORA_RANK),
                                        cache_kr[l].reshape(batch * past, QK_ROPE_DIM),
                                        w["wuk"], w["wuvt"], layer=l, batch=batch, seq=past)
            o_t = _sample_attn(q_t, k_past, v_past_t, k_new, v_new_t)
            o = o_t.transpose(0, 3, 1, 2).reshape(batch * seq, N_HEADS * V_HEAD_DIM)
        x = _mix_ln(o, y_conv, x, w["wgates"], b_gate, w["wmo"], w["wmix"], gain(1), bias(1), layer=l, tm=tm)
        x = _ffn_ln(x, w["ffn2_gu"], w["ffn2_d"], gain(2), bias(2), layer=l, tm=ffn_tm,
                    parts=ffn_parts)
    return (x.reshape(batch, seq, D_MODEL), lat.reshape(DEPTH, batch, seq, KV_LORA_RANK),
            kr.reshape(DEPTH, batch, seq, QK_ROPE_DIM), conv_new)


def kernel(x_prompt, x_sample, cache_kv_latent, cache_k_rope, state_conv, ffn1_w_gate_up, ffn1_w_down,
           ffn2_w_gate_up, ffn2_w_down, ln_gain, ln_bias, w_in, b_gate, q_norm_gain, kv_norm_gain, w_uq, w_ukv,
           w_mla_out, conv_w, w_conv_out, w_mix_out):
    w = _prepare_weights(ffn1_w_gate_up, ffn1_w_down, ffn2_w_gate_up, ffn2_w_down, w_in, w_uq, w_ukv,
                         w_mla_out, w_conv_out, w_mix_out)
    shared = (w, ln_gain, ln_bias, b_gate, q_norm_gain, kv_norm_gain, conv_w)
    y_prompt, p_lat, p_kr, p_conv = _run_trunk(x_prompt, None, None, None, *shared,
                                               tm=512, ffn_tm=1024, ffn_parts=4, seq_tm=512, attn_group=4)
    y_sample, s_lat, s_kr, s_conv = _run_trunk(x_sample, cache_kv_latent, cache_k_rope, state_conv, *shared,
                                               tm=512, ffn_tm=512, ffn_parts=2, seq_tm=x_sample.shape[1],
                                               attn_group=None)
    return (y_prompt, y_sample, p_lat, p_kr, p_conv, s_lat, s_kr, s_conv)
```
